```python
import math
import numpy as np
import jax
import jax.numpy as jnp
from jax import lax

D_MODEL = 2048
BATCH = 4
SEQ = 4096
DEPTH = 2

N_MIXERS = 2
N_A = (DEPTH + 1) // 2
N_B = DEPTH // 2
BLK = 128
EPS = 1e-6

MLA_HEADS = 16
Q_LORA = 512
KV_LORA = 512
NOPE_DIM = 128
ROPE_DIM = 64
V_DIM = 128
QK_DIM = NOPE_DIM + ROPE_DIM
ROPE_THETA = 10000.0

DIL_PAIRS = ((128, 1), (512, 4), (2048, 16))
DIL_GROUPS = len(DIL_PAIRS)
DIL_HEADS = 8
DIL_HEAD_DIM = 128
ALIBI_TOTAL_HEADS = DIL_GROUPS * DIL_HEADS

D_FF = 5632

kernel_name = "hybrid_mla_dilated_macaron"


def rmsnorm(t, g):
    tf = t.astype(jnp.float32)
    y = tf * lax.rsqrt(jnp.mean(tf * tf, axis=-1, keepdims=True) + EPS)
    return (y * g.astype(jnp.float32)).astype(t.dtype)


def swiglu(xn, w_in, w_out):
    gate, up = jnp.split(xn @ w_in, 2, axis=-1)
    return (jax.nn.silu(gate) * up) @ w_out


def rope_tables(S):
    inv = 1.0 / (ROPE_THETA ** (jnp.arange(0, ROPE_DIM, 2, dtype=jnp.float32) / ROPE_DIM))
    ang = jnp.arange(S, dtype=jnp.float32)[:, None] * inv[None, :]
    return jnp.cos(ang), jnp.sin(ang)


def apply_rope(t, cos, sin):
    t1, t2 = jnp.split(t, 2, axis=-1)
    c = cos[None, :, None, :].astype(t.dtype)
    s = sin[None, :, None, :].astype(t.dtype)
    return jnp.concatenate([t1 * c - t2 * s, t1 * s + t2 * c], axis=-1)


def mla_mixer(xn, w_down, g_cq, g_ckv, w_uq, w_ukv, g_qn, g_kn, w_o):
    B, S, _ = xn.shape
    lat = xn @ w_down
    c_q = rmsnorm(lat[..., :Q_LORA], g_cq)
    c_kv = rmsnorm(lat[..., Q_LORA:Q_LORA + KV_LORA], g_ckv)
    k_pe = lat[..., Q_LORA + KV_LORA:]
    q = (c_q @ w_uq).reshape(B, S, MLA_HEADS, QK_DIM)
    kv = (c_kv @ w_ukv).reshape(B, S, MLA_HEADS, NOPE_DIM + V_DIM)
    k_nope, v = kv[..., :NOPE_DIM], kv[..., NOPE_DIM:]
    k = jnp.concatenate(
        [k_nope, jnp.broadcast_to(k_pe[:, :, None, :], (B, S, MLA_HEADS, ROPE_DIM))], axis=-1)
    q = rmsnorm(q, g_qn)
    k = rmsnorm(k, g_kn)
    cos, sin = rope_tables(S)
    q = jnp.concatenate([q[..., :NOPE_DIM], apply_rope(q[..., NOPE_DIM:], cos, sin)], axis=-1)
    k = jnp.concatenate([k[..., :NOPE_DIM], apply_rope(k[..., NOPE_DIM:], cos, sin)], axis=-1)

    scale = 1.0 / math.sqrt(QK_DIM)
    nb = S // BLK
    qb = q.reshape(B, nb, BLK, MLA_HEADS, QK_DIM).transpose(1, 0, 2, 3, 4)
    starts = jnp.arange(nb, dtype=jnp.int32) * BLK
    kf = k.astype(jnp.float32)
    vf = v.astype(jnp.float32)
    kpos = jnp.arange(S, dtype=jnp.int32)

    def block(args):
        qblk, s0 = args
        s = jnp.einsum('bqhe,bkhe->bhqk', qblk.astype(jnp.float32), kf) * scale
        qpos = s0 + jnp.arange(BLK, dtype=jnp.int32)
        s = jnp.where((kpos[None, :] <= qpos[:, None])[None, None], s, -jnp.inf)
        p = jax.nn.softmax(s, axis=-1)
        return jnp.einsum('bhqk,bkhe->bqhe', p, vf)

    o = lax.map(block, (qb, starts))
    o = o.transpose(1, 0, 2, 3, 4).reshape(B, S, MLA_HEADS * V_DIM).astype(xn.dtype)
    return o @ w_o


def dilated_group_attn(q, k, v, window, dilation, slopes):
    B, S, H, dh = q.shape
    win_sub = window // dilation
    L = S // dilation
    Lp = -(-L // BLK) * BLK
    nb = Lp // BLK

    def to_sub(t):
        t = t.astype(jnp.float32).reshape(B, L, dilation, H, dh).transpose(0, 2, 1, 3, 4)
        return jnp.pad(t, ((0, 0), (0, 0), (0, Lp - L), (0, 0), (0, 0)))

    qb = to_sub(q).reshape(B, dilation, nb, BLK, H, dh)
    kb = to_sub(k).reshape(B, dilation, nb, BLK, H, dh)
    vb = to_sub(v).reshape(B, dilation, nb, BLK, H, dh)
    pad_prev = ((0, 0), (0, 0), (1, 0), (0, 0), (0, 0), (0, 0))
    kcat = jnp.concatenate([jnp.pad(kb, pad_prev)[:, :, :-1], kb], axis=3)
    vcat = jnp.concatenate([jnp.pad(vb, pad_prev)[:, :, :-1], vb], axis=3)

    iq = jnp.arange(BLK, dtype=jnp.int32)
    ik = jnp.arange(2 * BLK, dtype=jnp.int32)
    nidx = jnp.arange(nb, dtype=jnp.int32)
    dist = iq[:, None] + BLK - ik[None, :]
    key_ok = (nidx[:, None] * BLK - BLK + ik[None, :]) >= 0
    mask = ((dist >= 0) & (dist <= win_sub))[None] & key_ok[:, None, :]
    bias = -slopes[:, None, None] * (dilation * dist).astype(jnp.float32)[None]

    scale = 1.0 / math.sqrt(dh)
    s = jnp.einsum('bdnqhe,bdnkhe->bdnhqk', qb, kcat) * scale + bias[None, None, None]
    s = jnp.where(mask[None, None, :, None], s, -jnp.inf)
    lse = jax.nn.logsumexp(s, axis=-1)
    p = jnp.exp(s - lse[..., None])
    o = jnp.einsum('bdnhqk,bdnkhe->bdnqhe', p, vcat)

    o = o.reshape(B, dilation, Lp, H, dh)[:, :, :L].transpose(0, 2, 1, 3, 4).reshape(B, S, H, dh)
    lse = lse.transpose(0, 1, 2, 4, 3).reshape(B, dilation, Lp, H)[:, :, :L]
    lse = lse.transpose(0, 2, 1, 3).reshape(B, S, H)
    return o, lse


def alibi_slopes():
    k = np.arange(1, ALIBI_TOTAL_HEADS + 1, dtype=np.float32)
    return jnp.asarray(2.0 ** (-8.0 * k / ALIBI_TOTAL_HEADS), dtype=jnp.float32)


def dilated_mixer(xn, w_qkv, g_qn, g_kn, w_o):
    B, S, _ = xn.shape
    qkv = (xn @ w_qkv).reshape(B, S, 3, DIL_GROUPS, DIL_HEADS, DIL_HEAD_DIM)
    q = rmsnorm(qkv[:, :, 0], g_qn)
    k = rmsnorm(qkv[:, :, 1], g_kn)
    v = qkv[:, :, 2]
    slopes = alibi_slopes().reshape(DIL_GROUPS, DIL_HEADS)
    outs, lses = [], []
    for g, (window, dilation) in enumerate(DIL_PAIRS):
        o_g, lse_g = dilated_group_attn(q[:, :, g], k[:, :, g], v[:, :, g], window, dilation, slopes[g])
        outs.append(o_g)
        lses.append(lse_g)
    o = jnp.stack(outs, axis=2)
    w = jax.nn.softmax(jnp.stack(lses, axis=2), axis=2)
    o = jnp.sum(o * w[..., None], axis=2).reshape(B, S, DIL_HEADS * DIL_HEAD_DIM).astype(xn.dtype)
    return o @ w_o


def setup_inputs(seed: int = 0) -> dict:
    key = jax.random.key(seed)
    ks = iter(jax.random.split(key, 40))

    def w(shape, fan_in):
        return jax.random.normal(next(ks), shape, jnp.float32) * (fan_in ** -0.5)

    def gain(shape):
        return 1.0 + 0.02 * jax.random.normal(next(ks), shape, jnp.float32)

    D = D_MODEL
    return {
        "x": jax.random.normal(next(ks), (BATCH, SEQ, D), jnp.float32),
        "ffn1_norm": gain((DEPTH, D)),
        "ffn1_w_in": w((DEPTH, D, 2 * D_FF), D),
        "ffn1_w_out": w((DEPTH, D_FF, D), D_FF),
        "mix_norm": gain((DEPTH, D)),
        "ffn2_norm": gain((DEPTH, D)),
        "ffn2_w_in": w((DEPTH, D, 2 * D_FF), D),
        "ffn2_w_out": w((DEPTH, D_FF, D), D_FF),
        "mla_w_down": w((N_A, D, Q_LORA + KV_LORA + ROPE_DIM), D),
        "mla_g_cq": gain((N_A, Q_LORA)),
        "mla_g_ckv": gain((N_A, KV_LORA)),
        "mla_w_uq": w((N_A, Q_LORA, MLA_HEADS * QK_DIM), Q_LORA),
        "mla_w_ukv": w((N_A, KV_LORA, MLA_HEADS * (NOPE_DIM + V_DIM)), KV_LORA),
        "mla_g_qn": gain((N_A, QK_DIM)),
        "mla_g_kn": gain((N_A, QK_DIM)),
        "mla_w_o": w((N_A, MLA_HEADS * V_DIM, D), MLA_HEADS * V_DIM),
        "dil_w_qkv": w((N_B, D, 3 * DIL_GROUPS * DIL_HEADS * DIL_HEAD_DIM), D),
        "dil_g_qn": gain((N_B, DIL_HEAD_DIM)),
        "dil_g_kn": gain((N_B, DIL_HEAD_DIM)),
        "dil_w_o": w((N_B, DIL_HEADS * DIL_HEAD_DIM, D), DIL_HEADS * DIL_HEAD_DIM),
    }


def reference(x, ffn1_norm, ffn1_w_in, ffn1_w_out, mix_norm, ffn2_norm, ffn2_w_in, ffn2_w_out,
              mla_w_down, mla_g_cq, mla_g_ckv, mla_w_uq, mla_w_ukv, mla_g_qn, mla_g_kn, mla_w_o,
              dil_w_qkv, dil_g_qn, dil_g_kn, dil_w_o):
    for i in range(DEPTH):
        j = i // N_MIXERS
        x = x + 0.5 * swiglu(rmsnorm(x, ffn1_norm[i]), ffn1_w_in[i], ffn1_w_out[i])
        xn = rmsnorm(x, mix_norm[i])
        if i % N_MIXERS == 0:
            y = mla_mixer(xn, mla_w_down[j], mla_g_cq[j], mla_g_ckv[j], mla_w_uq[j], mla_w_ukv[j],
                          mla_g_qn[j], mla_g_kn[j], mla_w_o[j])
        else:
            y = dilated_mixer(xn, dil_w_qkv[j], dil_g_qn[j], dil_g_kn[j], dil_w_o[j])
        x = x + y
        x = x + 0.5 * swiglu(rmsnorm(x, ffn2_norm[i]), ffn2_w_in[i], ffn2_w_out[i])
    return x
```

```python
import functools
import math

import numpy as np
import jax
import jax.numpy as jnp
from jax import lax
from jax.experimental import pallas as pl
from jax.experimental.pallas import tpu as pltpu

EPS = 1e-6
ROPE_THETA = 10000.0
MASKED = -1e30

NOPE_DIM = 128
ROPE_DIM = 64
ROPE_HALF = ROPE_DIM // 2
V_DIM = 128
QK_DIM = NOPE_DIM + ROPE_DIM
QK_PAD = 256
DIL_PAIRS = ((128, 1), (512, 4), (2048, 16))
DIL_GROUPS = len(DIL_PAIRS)
DIL_HEADS = 8
DIL_HEAD_DIM = 128
DIL_BLK = 128
DIL_GROUP_W = DIL_HEADS * DIL_HEAD_DIM

LANES = 128
VMEM_LIMIT = 56 * 1024 * 1024

BF16 = jnp.bfloat16
F32 = jnp.float32


def _params(*semantics):
    return pltpu.CompilerParams(dimension_semantics=semantics, vmem_limit_bytes=VMEM_LIMIT)


def _rms(t):
    return t * lax.rsqrt(jnp.mean(t * t, axis=-1, keepdims=True) + EPS)


def _dot(a, b):
    return jnp.dot(a, b, preferred_element_type=F32)


def _dot_nt(a, b):
    return lax.dot_general(a, b, (((1,), (1,)), ((), ())), preferred_element_type=F32)


def _ffn_kernel(x_ref, g_ref, wg_ref, wu_ref, wo_ref, o_ref, xn_ref):
    j = pl.program_id(1)

    @pl.when(j == 0)
    def _():
        x = x_ref[...]
        xn_ref[...] = (_rms(x) * g_ref[...]).astype(BF16)
        o_ref[...] = x

    xn = xn_ref[...]
    gate = _dot(xn, wg_ref[...])
    up = _dot(xn, wu_ref[...])
    h = (gate * jax.nn.sigmoid(gate) * up).astype(BF16)
    o_ref[...] += 0.5 * _dot(h, wo_ref[...])


def _ffn(x, g, w_in, w_out, *, tm, tf):
    T, D = x.shape
    F = w_out.shape[0]
    nf = F // tf
    return pl.pallas_call(
        _ffn_kernel,
        out_shape=jax.ShapeDtypeStruct((T, D), F32),
        grid=(T // tm, nf),
        in_specs=[
            pl.BlockSpec((tm, D), lambda i, j: (i, 0)),
            pl.BlockSpec((1, D), lambda i, j: (0, 0)),
            pl.BlockSpec((D, tf), lambda i, j: (0, j)),
            pl.BlockSpec((D, tf), lambda i, j: (0, j + nf)),
            pl.BlockSpec((tf, D), lambda i, j: (j, 0)),
        ],
        out_specs=pl.BlockSpec((tm, D), lambda i, j: (i, 0)),
        scratch_shapes=[pltpu.VMEM((tm, D), BF16)],
        compiler_params=_params("parallel", "arbitrary"),
        name="ffn",
    )(x, g, w_in, w_in, w_out)


def _rope(t, cos_ref, sin_ref):
    return t * cos_ref[...] + pltpu.roll(t, LANES // 2, 1) * sin_ref[...]


def _mla_proj_kernel(x_ref, g_ref, wd_ref, gcq_ref, gckv_ref, wuq_ref, wukv_ref, gq_ref, gk_ref,
                     cos_ref, sin_ref, q_ref, k_ref, v_ref, *, heads, q_lora, kv_lora):
    xn = (_rms(x_ref[0]) * g_ref[...]).astype(BF16)
    lat = _dot(xn, wd_ref[...])
    cq = (_rms(lat[:, :q_lora]) * gcq_ref[...]).astype(BF16)
    ckv = (_rms(lat[:, q_lora:q_lora + kv_lora]) * gckv_ref[...]).astype(BF16)
    kpe = lat[:, q_lora + kv_lora:]
    kpe_ssq = jnp.sum(kpe * kpe, axis=-1, keepdims=True)
    kpe_rot = _rope(kpe * gk_ref[:, NOPE_DIM:], cos_ref, sin_ref)
    gq_n, gq_r = gq_ref[:, :NOPE_DIM], gq_ref[:, NOPE_DIM:]
    gk_n = gk_ref[:, :NOPE_DIM]
    for h in range(heads):
        cols = slice(h * QK_PAD, (h + 1) * QK_PAD)
        q = _dot(cq, wuq_ref[:, cols])
        qn, qr = q[:, :NOPE_DIM], q[:, NOPE_DIM:]
        ssq = jnp.sum(qn * qn + qr * qr, axis=-1, keepdims=True)
        rs = lax.rsqrt(ssq * (1.0 / QK_DIM) + EPS)
        q_ref[0, h, :, :NOPE_DIM] = (qn * rs * gq_n).astype(BF16)
        q_ref[0, h, :, NOPE_DIM:] = _rope(qr * rs * gq_r, cos_ref, sin_ref).astype(BF16)
        kv = _dot(ckv, wukv_ref[:, cols])
        kn = kv[:, :NOPE_DIM]
        ssk = jnp.sum(kn * kn, axis=-1, keepdims=True) + kpe_ssq
        rsk = lax.rsqrt(ssk * (1.0 / QK_DIM) + EPS)
        k_ref[0, h, :, :NOPE_DIM] = (kn * rsk * gk_n).astype(BF16)
        k_ref[0, h, :, NOPE_DIM:] = (kpe_rot * rsk).astype(BF16)
        v_ref[0, h] = kv[:, NOPE_DIM:].astype(BF16)


def _mla_proj(x, g, wd, gcq, gckv, wuq, wukv, gq, gk, cos, sin, *, heads, tm):
    B, S, D = x.shape
    q_lora, kv_lora = gcq.shape[1], gckv.shape[1]
    const = lambda b, i: (0, 0)
    kern = functools.partial(_mla_proj_kernel, heads=heads, q_lora=q_lora, kv_lora=kv_lora)
    return pl.pallas_call(
        kern,
        out_shape=(
            jax.ShapeDtypeStruct((B, heads, S, QK_PAD), BF16),
            jax.ShapeDtypeStruct((B, heads, S, QK_PAD), BF16),
            jax.ShapeDtypeStruct((B, heads, S, V_DIM), BF16),
        ),
        grid=(B, S // tm),
        in_specs=[
            pl.BlockSpec((1, tm, D), lambda b, i: (b, i, 0)),
            pl.BlockSpec(g.shape, const),
            pl.BlockSpec(wd.shape, const),
            pl.BlockSpec(gcq.shape, const),
            pl.BlockSpec(gckv.shape, const),
            pl.BlockSpec(wuq.shape, const),
            pl.BlockSpec(wukv.shape, const),
            pl.BlockSpec(gq.shape, const),
            pl.BlockSpec(gk.shape, const),
            pl.BlockSpec((tm, LANES), lambda b, i: (i, 0)),
            pl.BlockSpec((tm, LANES), lambda b, i: (i, 0)),
        ],
        out_specs=(
            pl.BlockSpec((1, heads, tm, QK_PAD), lambda b, i: (b, 0, i, 0)),
            pl.BlockSpec((1, heads, tm, QK_PAD), lambda b, i: (b, 0, i, 0)),
            pl.BlockSpec((1, heads, tm, V_DIM), lambda b, i: (b, 0, i, 0)),
        ),
        compiler_params=_params("parallel", "parallel"),
        name="mla_proj",
    )(x, g, wd, gcq, gckv, wuq, wukv, gq, gk, cos, sin)


def _mla_attn_kernel(q_ref, k_ref, v_ref, o_ref, m_ref, l_ref, acc_ref, *, tq):
    qi = pl.program_id(2)
    q = q_ref[0, 0]
    m_ref[...] = jnp.full(m_ref.shape, MASKED, F32)
    l_ref[...] = jnp.zeros(l_ref.shape, F32)
    acc_ref[...] = jnp.zeros(acc_ref.shape, F32)

    def step(kc, causal):
        rows = pl.ds(pl.multiple_of(kc * tq, tq), tq)
        s = _dot_nt(q, k_ref[0, 0, rows, :])
        if causal:
            r = lax.broadcasted_iota(jnp.int32, s.shape, 0)
            c = lax.broadcasted_iota(jnp.int32, s.shape, 1)
            s = jnp.where(c <= r, s, MASKED)
        m_old = m_ref[...]
        m_new = jnp.maximum(m_old, jnp.max(s, axis=-1, keepdims=True))
        alpha = jnp.exp(m_old - m_new)
        p = jnp.exp(s - m_new)
        l_ref[...] = alpha * l_ref[...] + jnp.sum(p, axis=-1, keepdims=True)
        acc_ref[...] = alpha * acc_ref[...] + _dot(p.astype(BF16), v_ref[0, 0, rows, :])
        m_ref[...] = m_new

    def body(kc, carry):
        step(kc, False)
        return carry

    lax.fori_loop(0, qi, body, 0)
    step(qi, True)
    o_ref[0] = (acc_ref[...] / l_ref[...]).astype(o_ref.dtype)


def _mla_attn(q, k, v, *, tq):
    B, H, S, _ = q.shape
    return pl.pallas_call(
        functools.partial(_mla_attn_kernel, tq=tq),
        out_shape=jax.ShapeDtypeStruct((B, S, H * V_DIM), BF16),
        grid=(B, H, S // tq),
        in_specs=[
            pl.BlockSpec((1, 1, tq, QK_PAD), lambda b, h, i: (b, h, i, 0)),
            pl.BlockSpec((1, 1, S, QK_PAD), lambda b, h, i: (b, h, 0, 0)),
            pl.BlockSpec((1, 1, S, V_DIM), lambda b, h, i: (b, h, 0, 0)),
        ],
        out_specs=pl.BlockSpec((1, tq, V_DIM), lambda b, h, i: (b, i, h)),
        scratch_shapes=[
            pltpu.VMEM((tq, 1), F32),
            pltpu.VMEM((tq, 1), F32),
            pltpu.VMEM((tq, V_DIM), F32),
        ],
        compiler_params=_params("parallel", "parallel", "arbitrary"),
        name="mla_attn",
    )(q, k, v)


def _out_proj_kernel(x_ref, o_ref, w_ref, out_ref):
    out_ref[...] = x_ref[...] + _dot(o_ref[...], w_ref[...])


def _out_proj(x, o, w, *, tm):
    T, D = x.shape
    return pl.pallas_call(
        _out_proj_kernel,
        out_shape=jax.ShapeDtypeStruct((T, D), F32),
        grid=(T // tm,),
        in_specs=[
            pl.BlockSpec((tm, D), lambda i: (i, 0)),
            pl.BlockSpec((tm, o.shape[1]), lambda i: (i, 0)),
            pl.BlockSpec(w.shape, lambda i: (0, 0)),
        ],
        out_specs=pl.BlockSpec((tm, D), lambda i: (i, 0)),
        compiler_params=_params("parallel"),
        name="out_proj",
    )(x, o, w)


def _dil_qkv_kernel(x_ref, g_ref, w_ref, gain_ref, o_ref, xn_ref):
    j = pl.program_id(1)

    @pl.when(j == 0)
    def _():
        xn_ref[...] = (_rms(x_ref[...]) * g_ref[...]).astype(BF16)

    y = _dot(xn_ref[...], w_ref[...])

    @pl.when(j < 2 * DIL_GROUPS)
    def _():
        for h in range(DIL_HEADS):
            cols = slice(h * DIL_HEAD_DIM, (h + 1) * DIL_HEAD_DIM)
            o_ref[:, cols] = (_rms(y[:, cols]) * gain_ref[0]).astype(BF16)

    @pl.when(j >= 2 * DIL_GROUPS)
    def _():
        o_ref[...] = y.astype(BF16)


def _dil_qkv(x, g, w, gains, *, tm):
    T, D = x.shape
    N = w.shape[1]
    return pl.pallas_call(
        _dil_qkv_kernel,
        out_shape=jax.ShapeDtypeStruct((T, N), BF16),
        grid=(T // tm, N // DIL_GROUP_W),
        in_specs=[
            pl.BlockSpec((tm, D), lambda i, j: (i, 0)),
            pl.BlockSpec((1, D), lambda i, j: (0, 0)),
            pl.BlockSpec((D, DIL_GROUP_W), lambda i, j: (0, j)),
            pl.BlockSpec((1, 1, DIL_HEAD_DIM), lambda i, j: (j, 0, 0)),
        ],
        out_specs=pl.BlockSpec((tm, DIL_GROUP_W), lambda i, j: (i, j)),
        scratch_shapes=[pltpu.VMEM((tm, D), BF16)],
        compiler_params=_params("parallel", "arbitrary"),
        name="dil_qkv",
    )(x, g, w, gains)


def _dil_attn_kernel(q_ref, k_ref, kp_ref, v_ref, vp_ref, bias_ref, o_ref, lse_ref, *, tl):
    i = pl.program_id(2)
    first_pen = jnp.where(i == 0, MASKED, 0.0).astype(F32)
    lane = lax.broadcasted_iota(jnp.int32, (DIL_BLK, LANES), 1)
    for qb in range(tl // DIL_BLK):
        rows = slice(qb * DIL_BLK, (qb + 1) * DIL_BLK)
        lse_tile = jnp.zeros((DIL_BLK, LANES), F32)
        for h in range(DIL_HEADS):
            cols = slice(h * DIL_HEAD_DIM, (h + 1) * DIL_HEAD_DIM)
            q = q_ref[0, rows, cols]
            if qb == 0:
                s_prev = _dot_nt(q, kp_ref[0, :, cols]) + first_pen
                s = jnp.concatenate([s_prev, _dot_nt(q, k_ref[0, rows, cols])], axis=1)
            else:
                s = _dot_nt(q, k_ref[0, (qb - 1) * DIL_BLK:(qb + 1) * DIL_BLK, cols])
            s = s + bias_ref[h]
            m = jnp.max(s, axis=-1, keepdims=True)
            p = jnp.exp(s - m)
            l = jnp.sum(p, axis=-1, keepdims=True)
            pb = p.astype(BF16)
            if qb == 0:
                o = _dot(pb[:, :DIL_BLK], vp_ref[0, :, cols]) + _dot(pb[:, DIL_BLK:], v_ref[0, rows, cols])
            else:
                o = _dot(pb, v_ref[0, (qb - 1) * DIL_BLK:(qb + 1) * DIL_BLK, cols])
            o_ref[0, rows, cols] = (o / l).astype(o_ref.dtype)
            lse_tile = jnp.where(lane == h, m + jnp.log(l), lse_tile)
        lse_ref[0, rows, :] = lse_tile


def _dil_attn(qkv, bias, *, group, dilation, tl):
    B, S, N = qkv.shape
    L = S // dilation
    ncol = N // DIL_GROUP_W
    view = qkv.reshape(B, L, dilation * N)
    bpt = tl // DIL_BLK
    qcol = lambda b, r, i: (b, i, r * ncol + group)
    kcol = lambda b, r, i: (b, i, r * ncol + DIL_GROUPS + group)
    vcol = lambda b, r, i: (b, i, r * ncol + 2 * DIL_GROUPS + group)
    kprev = lambda b, r, i: (b, jnp.maximum(i * bpt - 1, 0), r * ncol + DIL_GROUPS + group)
    vprev = lambda b, r, i: (b, jnp.maximum(i * bpt - 1, 0), r * ncol + 2 * DIL_GROUPS + group)
    o, lse = pl.pallas_call(
        functools.partial(_dil_attn_kernel, tl=tl),
        out_shape=(
            jax.ShapeDtypeStruct((B, L, dilation * DIL_GROUP_W), BF16),
            jax.ShapeDtypeStruct((B, L, dilation * LANES), F32),
        ),
        grid=(B, dilation, L // tl),
        in_specs=[
            pl.BlockSpec((1, tl, DIL_GROUP_W), qcol),
            pl.BlockSpec((1, tl, DIL_GROUP_W), kcol),
            pl.BlockSpec((1, DIL_BLK, DIL_GROUP_W), kprev),
            pl.BlockSpec((1, tl, DIL_GROUP_W), vcol),
            pl.BlockSpec((1, DIL_BLK, DIL_GROUP_W), vprev),
            pl.BlockSpec(bias.shape, lambda b, r, i: (0, 0, 0)),
        ],
        out_specs=(
            pl.BlockSpec((1, tl, DIL_GROUP_W), lambda b, r, i: (b, i, r)),
            pl.BlockSpec((1, tl, LANES), lambda b, r, i: (b, i, r)),
        ),
        compiler_params=_params("parallel", "parallel", "arbitrary"),
        name=f"dil_attn_g{group}",
    )(view, view, view, view, view, bias)
    return o.reshape(B * S, DIL_GROUP_W), lse.reshape(B * S, LANES)


def _dil_out_kernel(x_ref, o0_ref, o1_ref, o2_ref, l0_ref, l1_ref, l2_ref, w_ref, out_ref, mrg_ref):
    lses = (l0_ref[...], l1_ref[...], l2_ref[...])
    mx = jnp.maximum(jnp.maximum(lses[0], lses[1]), lses[2])
    es = [jnp.exp(t - mx) for t in lses]
    den = es[0] + es[1] + es[2]
    ws = [e / den for e in es]
    o_refs = (o0_ref, o1_ref, o2_ref)
    for h in range(DIL_HEADS):
        cols = slice(h * DIL_HEAD_DIM, (h + 1) * DIL_HEAD_DIM)
        acc = None
        for g in range(DIL_GROUPS):
            term = ws[g][:, h:h + 1] * o_refs[g][:, cols].astype(F32)
            acc = term if acc is None else acc + term
        mrg_ref[:, cols] = acc.astype(BF16)
    out_ref[...] = x_ref[...] + _dot(mrg_ref[...], w_ref[...])


def _dil_out(x, os, lses, w, *, tm):
    T, D = x.shape
    row = lambda i: (i, 0)
    return pl.pallas_call(
        _dil_out_kernel,
        out_shape=jax.ShapeDtypeStruct((T, D), F32),
        grid=(T // tm,),
        in_specs=[pl.BlockSpec((tm, D), row)]
        + [pl.BlockSpec((tm, DIL_GROUP_W), row)] * DIL_GROUPS
        + [pl.BlockSpec((tm, LANES), row)] * DIL_GROUPS
        + [pl.BlockSpec(w.shape, lambda i: (0, 0))],
        out_specs=pl.BlockSpec((tm, D), row),
        scratch_shapes=[pltpu.VMEM((tm, DIL_GROUP_W), BF16)],
        compiler_params=_params("parallel"),
        name="dil_out",
    )(x, *os, *lses, w)


def _pad_rope(t):
    z = jnp.zeros(t.shape[:-1] + (LANES // 2 - ROPE_HALF,), t.dtype)
    return jnp.concatenate([t[..., :ROPE_HALF], z, t[..., ROPE_HALF:], z], axis=-1)


def _pad_qk(t):
    return jnp.concatenate([t[..., :NOPE_DIM], _pad_rope(t[..., NOPE_DIM:])], axis=-1)


def _rope_tables(S):
    inv = 1.0 / (ROPE_THETA ** (jnp.arange(0, ROPE_DIM, 2, dtype=F32) / ROPE_DIM))
    ang = jnp.arange(S, dtype=F32)[:, None] * inv[None, :]
    cos, sin = jnp.cos(ang), jnp.sin(ang)
    return _pad_rope(jnp.concatenate([cos, cos], -1)), _pad_rope(jnp.concatenate([-sin, sin], -1))


def _dil_bias(group, dilation):
    total = DIL_GROUPS * DIL_HEADS
    slopes = 2.0 ** (-8.0 * np.arange(1, total + 1, dtype=np.float32) / total)
    slopes = slopes.reshape(DIL_GROUPS, DIL_HEADS)[group]
    window = DIL_PAIRS[group][0] // dilation
    dist = np.arange(DIL_BLK)[:, None] + DIL_BLK - np.arange(2 * DIL_BLK)[None, :]
    ok = (dist >= 0) & (dist <= window)
    bias = -slopes[:, None, None] * (dilation * dist).astype(np.float32)[None]
    return jnp.asarray(np.where(ok[None], bias, np.float32(MASKED)), dtype=F32)


def _tile(n, want):
    t = min(n, want)
    assert n % t == 0, (n, want)
    return t


def kernel(x, ffn1_norm, ffn1_w_in, ffn1_w_out, mix_norm, ffn2_norm, ffn2_w_in, ffn2_w_out,
           mla_w_down, mla_g_cq, mla_g_ckv, mla_w_uq, mla_w_ukv, mla_g_qn, mla_g_kn, mla_w_o,
           dil_w_qkv, dil_g_qn, dil_g_kn, dil_w_o):
    B, S, D = x.shape
    T = B * S
    depth = ffn1_norm.shape[0]
    F = ffn1_w_out.shape[1]
    tm = _tile(T, 512)
    tf = _tile(F, 512)
    row = lambda v: v.reshape(1, -1).astype(F32)

    def ffn(xt, g, w_in, w_out):
        return _ffn(xt, row(g), w_in.astype(BF16), w_out.astype(BF16), tm=tm, tf=tf)

    xt = x.reshape(T, D)
    for i in range(depth):
        j = i // 2
        xt = ffn(xt, ffn1_norm[i], ffn1_w_in[i], ffn1_w_out[i])
        if i % 2 == 0:
            q_lora, kv_lora = mla_g_cq.shape[1], mla_g_ckv.shape[1]
            heads = mla_w_uq.shape[2] // QK_DIM
            wd = mla_w_down[j]
            wd = jnp.concatenate([wd[:, :q_lora + kv_lora], _pad_rope(wd[:, q_lora + kv_lora:])], -1)
            wuq = _pad_qk(mla_w_uq[j].reshape(q_lora, heads, QK_DIM)).reshape(q_lora, heads * QK_PAD)
            cos, sin = _rope_tables(S)
            q, k, v = _mla_proj(
                xt.reshape(B, S, D), row(mix_norm[i]), wd.astype(BF16), row(mla_g_cq[j]), row(mla_g_ckv[j]),
                wuq.astype(BF16), mla_w_ukv[j].astype(BF16),
                row(_pad_qk(mla_g_qn[j]) * (1.0 / math.sqrt(QK_DIM))), row(_pad_qk(mla_g_kn[j])),
                cos, sin, heads=heads, tm=_tile(S, 256))
            o = _mla_attn(q, k, v, tq=_tile(S, 512))
            xt = _out_proj(xt, o.reshape(T, heads * V_DIM), mla_w_o[j].astype(BF16), tm=tm)
        else:
            gains = jnp.concatenate([
                jnp.broadcast_to(dil_g_qn[j] * (1.0 / math.sqrt(DIL_HEAD_DIM)), (DIL_GROUPS, DIL_HEAD_DIM)),
                jnp.broadcast_to(dil_g_kn[j], (DIL_GROUPS, DIL_HEAD_DIM)),
                jnp.ones((DIL_GROUPS, DIL_HEAD_DIM), F32)], 0).reshape(3 * DIL_GROUPS, 1, DIL_HEAD_DIM)
            qkv = _dil_qkv(xt, row(mix_norm[i]), dil_w_qkv[j].astype(BF16), gains, tm=_tile(T, 1024))
            os, lses = [], []
            for g, (_, dilation) in enumerate(DIL_PAIRS):
                o_g, lse_g = _dil_attn(qkv.reshape(B, S, -1), _dil_bias(g, dilation), group=g,
                                       dilation=dilation, tl=_tile(S // dilation, 512))
                os.append(o_g)
                lses.append(lse_g)
            xt = _dil_out(xt, os, lses, dil_w_o[j].astype(BF16), tm=tm)
        xt = ffn(xt, ffn2_norm[i], ffn2_w_in[i], ffn2_w_out[i])
    return xt.reshape(B, S, D)
```

```python
import functools
import math

import numpy as np
import jax
import jax.numpy as jnp
from jax import lax
from jax.experimental import pallas as pl
from jax.experimental.pallas import tpu as pltpu

EPS = 1e-6
ROPE_THETA = 10000.0
MASKED = -1e30

NOPE_DIM = 128
ROPE_DIM = 64
ROPE_HALF = ROPE_DIM // 2
V_DIM = 128
QK_DIM = NOPE_DIM + ROPE_DIM
QK_PAD = 256
DIL_PAIRS = ((128, 1), (512, 4), (2048, 16))
DIL_GROUPS = len(DIL_PAIRS)
DIL_HEADS = 8
DIL_HEAD_DIM = 128
DIL_BLK = 128
DIL_GROUP_W = DIL_HEADS * DIL_HEAD_DIM

LANES = 128
VMEM_LIMIT = 56 * 1024 * 1024

BF16 = jnp.bfloat16
F32 = jnp.float32


def _params(*semantics):
    return pltpu.CompilerParams(dimension_semantics=semantics, vmem_limit_bytes=VMEM_LIMIT)


def _rms(t):
    return t * lax.rsqrt(jnp.mean(t * t, axis=-1, keepdims=True) + EPS)


def _dot(a, b):
    return jnp.dot(a, b, preferred_element_type=F32)


def _dot_nt(a, b):
    return lax.dot_general(a, b, (((1,), (1,)), ((), ())), preferred_element_type=F32)


def _ffn_kernel(x_ref, g_ref, wg_ref, wu_ref, wo_ref, o_ref, xn_ref):
    j = pl.program_id(1)

    @pl.when(j == 0)
    def _():
        x = x_ref[...]
        xn_ref[...] = (_rms(x) * g_ref[...]).astype(BF16)
        o_ref[...] = x

    xn = xn_ref[...]
    gate = _dot(xn, wg_ref[...])
    up = _dot(xn, wu_ref[...])
    h = (gate * jax.nn.sigmoid(gate) * up).astype(BF16)
    o_ref[...] += 0.5 * _dot(h, wo_ref[...])


def _ffn(x, g, w_in, w_out, *, tm, tf):
    T, D = x.shape
    F = w_out.shape[0]
    nf = F // tf
    return pl.pallas_call(
        _ffn_kernel,
        out_shape=jax.ShapeDtypeStruct((T, D), F32),
        grid=(T // tm, nf),
        in_specs=[
            pl.BlockSpec((tm, D), lambda i, j: (i, 0)),
            pl.BlockSpec((1, D), lambda i, j: (0, 0)),
            pl.BlockSpec((D, tf), lambda i, j: (0, j)),
            pl.BlockSpec((D, tf), lambda i, j: (0, j + nf)),
            pl.BlockSpec((tf, D), lambda i, j: (j, 0)),
        ],
        out_specs=pl.BlockSpec((tm, D), lambda i, j: (i, 0)),
        scratch_shapes=[pltpu.VMEM((tm, D), BF16)],
        compiler_params=_params("parallel", "arbitrary"),
        name="ffn",
    )(x, g, w_in, w_in, w_out)


def _rope(t, cos_ref, sin_ref):
    return t * cos_ref[...] + pltpu.roll(t, LANES // 2, 1) * sin_ref[...]


def _mla_proj_kernel(x_ref, g_ref, wd_ref, gcq_ref, gckv_ref, wuq_ref, wukv_ref, gq_ref, gk_ref,
                     cos_ref, sin_ref, q_ref, k_ref, v_ref, *, heads, q_lora, kv_lora):
    xn = (_rms(x_ref[0]) * g_ref[...]).astype(BF16)
    lat = _dot(xn, wd_ref[...])
    cq = (_rms(lat[:, :q_lora]) * gcq_ref[...]).astype(BF16)
    ckv = (_rms(lat[:, q_lora:q_lora + kv_lora]) * gckv_ref[...]).astype(BF16)
    kpe = lat[:, q_lora + kv_lora:]
    kpe_ssq = jnp.sum(kpe * kpe, axis=-1, keepdims=True)
    kpe_rot = _rope(kpe * gk_ref[:, NOPE_DIM:], cos_ref, sin_ref)
    gq_n, gq_r = gq_ref[:, :NOPE_DIM], gq_ref[:, NOPE_DIM:]
    gk_n = gk_ref[:, :NOPE_DIM]
    for h in range(heads):
        cols = slice(h * QK_PAD, (h + 1) * QK_PAD)
        q = _dot(cq, wuq_ref[:, cols])
        qn, qr = q[:, :NOPE_DIM], q[:, NOPE_DIM:]
        ssq = jnp.sum(qn * qn + qr * qr, axis=-1, keepdims=True)
        rs = lax.rsqrt(ssq * (1.0 / QK_DIM) + EPS)
        q_ref[0, h, :, :NOPE_DIM] = (qn * rs * gq_n).astype(BF16)
        q_ref[0, h, :, NOPE_DIM:] = _rope(qr * rs * gq_r, cos_ref, sin_ref).astype(BF16)
        kv = _dot(ckv, wukv_ref[:, cols])
        kn = kv[:, :NOPE_DIM]
        ssk = jnp.sum(kn * kn, axis=-1, keepdims=True) + kpe_ssq
        rsk = lax.rsqrt(ssk * (1.0 / QK_DIM) + EPS)
        k_ref[0, h, :, :NOPE_DIM] = (kn * rsk * gk_n).astype(BF16)
        k_ref[0, h, :, NOPE_DIM:] = (kpe_rot * rsk).astype(BF16)
        v_ref[0, h] = kv[:, NOPE_DIM:].astype(BF16)


def _mla_proj(x, g, wd, gcq, gckv, wuq, wukv, gq, gk, cos, sin, *, heads, tm):
    B, S, D = x.shape
    q_lora, kv_lora = gcq.shape[1], gckv.shape[1]
    const = lambda b, i: (0, 0)
    kern = functools.partial(_mla_proj_kernel, heads=heads, q_lora=q_lora, kv_lora=kv_lora)
    return pl.pallas_call(
        kern,
        out_shape=(
            jax.ShapeDtypeStruct((B, heads, S, QK_PAD), BF16),
            jax.ShapeDtypeStruct((B, heads, S, QK_PAD), BF16),
            jax.ShapeDtypeStruct((B, heads, S, V_DIM), BF16),
        ),
        grid=(B, S // tm),
        in_specs=[
            pl.BlockSpec((1, tm, D), lambda b, i: (b, i, 0)),
            pl.BlockSpec(g.shape, const),
            pl.BlockSpec(wd.shape, const),
            pl.BlockSpec(gcq.shape, const),
            pl.BlockSpec(gckv.shape, const),
            pl.BlockSpec(wuq.shape, const),
            pl.BlockSpec(wukv.shape, const),
            pl.BlockSpec(gq.shape, const),
            pl.BlockSpec(gk.shape, const),
            pl.BlockSpec((tm, LANES), lambda b, i: (i, 0)),
            pl.BlockSpec((tm, LANES), lambda b, i: (i, 0)),
        ],
        out_specs=(
            pl.BlockSpec((1, heads, tm, QK_PAD), lambda b, i: (b, 0, i, 0)),
            pl.BlockSpec((1, heads, tm, QK_PAD), lambda b, i: (b, 0, i, 0)),
            pl.BlockSpec((1, heads, tm, V_DIM), lambda b, i: (b, 0, i, 0)),
        ),
        compiler_params=_params("parallel", "parallel"),
        name="mla_proj",
    )(x, g, wd, gcq, gckv, wuq, wukv, gq, gk, cos, sin)


def _mla_attn_kernel(q_ref, k_ref, v_ref, o_ref, m_ref, l_ref, acc_ref, *, tq, tk, sub):
    qi = pl.program_id(2)
    m_ref[...] = jnp.full(m_ref.shape, MASKED, F32)
    l_ref[...] = jnp.zeros(l_ref.shape, F32)
    acc_ref[...] = jnp.zeros(acc_ref.shape, F32)

    def chain(sb, k, v, causal):
        rows = slice(sb * sub, (sb + 1) * sub)
        s = _dot_nt(q_ref[0, 0, rows, :], k)
        if causal:
            r = lax.broadcasted_iota(jnp.int32, s.shape, 0)
            c = lax.broadcasted_iota(jnp.int32, s.shape, 1)
            s = jnp.where(c <= r, s, MASKED)
        blocks = [s[:, c * LANES:(c + 1) * LANES] for c in range(s.shape[1] // LANES)]
        m_old = m_ref[rows, :]
        m_new = jnp.maximum(m_old, jnp.max(functools.reduce(jnp.maximum, blocks), axis=-1, keepdims=True))
        alpha = jnp.exp(m_old - m_new)
        ps = [jnp.exp(blk - m_new) for blk in blocks]
        l_ref[rows, :] = alpha * l_ref[rows, :] + jnp.sum(functools.reduce(jnp.add, ps), axis=-1, keepdims=True)
        p = jnp.concatenate(ps, axis=1).astype(BF16)
        acc_ref[rows, :] = alpha * acc_ref[rows, :] + _dot(p, v)
        m_ref[rows, :] = m_new

    def body(kc, carry):
        keys = pl.ds(pl.multiple_of(kc * tk, tk), tk)
        k, v = k_ref[0, 0, keys, :], v_ref[0, 0, keys, :]
        for sb in range(tq // sub):
            chain(sb, k, v, False)
        return carry

    lax.fori_loop(0, qi * (tq // tk), body, 0)
    for sb in range(tq // sub):
        for kb in range(sb + 1):
            keys = pl.ds(pl.multiple_of(qi * tq + kb * sub, sub), sub)
            chain(sb, k_ref[0, 0, keys, :], v_ref[0, 0, keys, :], kb == sb)
    o_ref[0] = (acc_ref[...] / l_ref[...]).astype(o_ref.dtype)


def _mla_attn(q, k, v, *, tq, tk, sub):
    B, H, S, _ = q.shape
    return pl.pallas_call(
        functools.partial(_mla_attn_kernel, tq=tq, tk=tk, sub=sub),
        out_shape=jax.ShapeDtypeStruct((B, S, H * V_DIM), BF16),
        grid=(B, H, S // tq),
        in_specs=[
            pl.BlockSpec((1, 1, tq, QK_PAD), lambda b, h, i: (b, h, i, 0)),
            pl.BlockSpec((1, 1, S, QK_PAD), lambda b, h, i: (b, h, 0, 0)),
            pl.BlockSpec((1, 1, S, V_DIM), lambda b, h, i: (b, h, 0, 0)),
        ],
        out_specs=pl.BlockSpec((1, tq, V_DIM), lambda b, h, i: (b, i, h)),
        scratch_shapes=[
            pltpu.VMEM((tq, LANES), F32),
            pltpu.VMEM((tq, LANES), F32),
            pltpu.VMEM((tq, V_DIM), F32),
        ],
        compiler_params=_params("parallel", "parallel", "arbitrary"),
        name="mla_attn",
    )(q, k, v)


def _out_proj_kernel(x_ref, o_ref, w_ref, out_ref):
    out_ref[...] = x_ref[...] + _dot(o_ref[...], w_ref[...])


def _out_proj(x, o, w, *, tm):
    T, D = x.shape
    return pl.pallas_call(
        _out_proj_kernel,
        out_shape=jax.ShapeDtypeStruct((T, D), F32),
        grid=(T // tm,),
        in_specs=[
            pl.BlockSpec((tm, D), lambda i: (i, 0)),
            pl.BlockSpec((tm, o.shape[1]), lambda i: (i, 0)),
            pl.BlockSpec(w.shape, lambda i: (0, 0)),
        ],
        out_specs=pl.BlockSpec((tm, D), lambda i: (i, 0)),
        compiler_params=_params("parallel"),
        name="out_proj",
    )(x, o, w)


def _dil_qkv_kernel(x_ref, g_ref, w_ref, gain_ref, o0_ref, o1_ref, o2_ref, xn_ref, y_ref):
    j = pl.program_id(1)
    tm = x_ref.shape[0]

    @pl.when(j == 0)
    def _():
        xn_ref[...] = (_rms(x_ref[...]) * g_ref[...]).astype(BF16)

    y = _dot(xn_ref[...], w_ref[...])
    is_v = j % 3 == 2

    @pl.when(jnp.logical_not(is_v))
    def _():
        for h in range(DIL_HEADS):
            cols = slice(h * DIL_HEAD_DIM, (h + 1) * DIL_HEAD_DIM)
            y_ref[h] = _rms(y[:, cols]) * gain_ref[0]

    @pl.when(is_v)
    def _():
        for h in range(DIL_HEADS):
            y_ref[h] = y[:, h * DIL_HEAD_DIM:(h + 1) * DIL_HEAD_DIM]

    for group, (o_ref, (_, d)) in enumerate(zip((o0_ref, o1_ref, o2_ref), DIL_PAIRS)):
        @pl.when(j // 3 == group)
        def _(o_ref=o_ref, d=d):
            for h in range(DIL_HEADS):
                cols = slice(h * DIL_HEAD_DIM, (h + 1) * DIL_HEAD_DIM)
                for r in range(d):
                    rows = pl.ds(r, tm // d, stride=d) if d > 1 else slice(None)
                    o_ref[0, 0, r, :, cols] = y_ref[h, rows, :].astype(BF16)


def _dil_qkv(x, g, w, gains, *, batch, tm):
    T, D = x.shape
    S = T // batch
    tiles = S // tm
    wcol = lambda i, j: (0, (j % 3) * DIL_GROUPS + j // 3)
    out_shapes, out_specs = [], []
    for group, (_, d) in enumerate(DIL_PAIRS):
        out_shapes.append(jax.ShapeDtypeStruct((3, batch, d, S // d, DIL_GROUP_W), BF16))
        out_specs.append(pl.BlockSpec(
            (1, 1, d, tm // d, DIL_GROUP_W),
            lambda i, j, group=group: (jnp.clip(j - 3 * group, 0, 2), i // tiles, 0, i % tiles, 0)))
    return pl.pallas_call(
        _dil_qkv_kernel,
        out_shape=tuple(out_shapes),
        grid=(T // tm, 3 * DIL_GROUPS),
        in_specs=[
            pl.BlockSpec((tm, D), lambda i, j: (i, 0)),
            pl.BlockSpec((1, D), lambda i, j: (0, 0)),
            pl.BlockSpec((D, DIL_GROUP_W), wcol),
            pl.BlockSpec((1, 1, DIL_HEAD_DIM), lambda i, j: (j % 3, 0, 0)),
        ],
        out_specs=tuple(out_specs),
        scratch_shapes=[pltpu.VMEM((tm, D), BF16), pltpu.VMEM((DIL_HEADS, tm, DIL_HEAD_DIM), F32)],
        compiler_params=_params("arbitrary", "arbitrary"),
        name="dil_qkv",
    )(x, g, w, gains)


def _dil_attn_kernel(q_ref, k_ref, kp_ref, v_ref, vp_ref, bias_ref, o_ref, lse_ref, *, tl):
    i = pl.program_id(2)
    first_pen = jnp.where(i == 0, MASKED, 0.0).astype(F32)
    lane = lax.broadcasted_iota(jnp.int32, (DIL_BLK, LANES), 1)
    q_ref, k_ref, kp_ref, v_ref, vp_ref = (t.at[0, 0, 0] for t in (q_ref, k_ref, kp_ref, v_ref, vp_ref))
    for qb in range(tl // DIL_BLK):
        rows = slice(qb * DIL_BLK, (qb + 1) * DIL_BLK)
        lse_tile = jnp.zeros((DIL_BLK, LANES), F32)
        for h in range(DIL_HEADS):
            cols = slice(h * DIL_HEAD_DIM, (h + 1) * DIL_HEAD_DIM)
            q = q_ref[rows, cols]
            if qb == 0:
                s_prev = _dot_nt(q, kp_ref[:, cols]) + first_pen
                s = jnp.concatenate([s_prev, _dot_nt(q, k_ref[rows, cols])], axis=1)
            else:
                s = _dot_nt(q, k_ref[(qb - 1) * DIL_BLK:(qb + 1) * DIL_BLK, cols])
            s = s + bias_ref[h]
            m = jnp.max(s, axis=-1, keepdims=True)
            p = jnp.exp(s - m)
            l = jnp.sum(p, axis=-1, keepdims=True)
            pb = p.astype(BF16)
            if qb == 0:
                o = _dot(pb[:, :DIL_BLK], vp_ref[:, cols]) + _dot(pb[:, DIL_BLK:], v_ref[rows, cols])
            else:
                o = _dot(pb, v_ref[(qb - 1) * DIL_BLK:(qb + 1) * DIL_BLK, cols])
            o_ref[0, rows, cols] = (o / l).astype(o_ref.dtype)
            lse_tile = jnp.where(lane == h, m + jnp.log(l), lse_tile)
        lse_ref[0, rows, :] = lse_tile


def _dil_attn(qkv, bias, *, tl):
    _, B, d, L, _ = qkv.shape
    bpt = tl // DIL_BLK
    cur = lambda which: pl.BlockSpec((1, 1, 1, tl, DIL_GROUP_W), lambda b, r, i: (which, b, r, i, 0))
    prev = lambda which: pl.BlockSpec((1, 1, 1, DIL_BLK, DIL_GROUP_W),
                                      lambda b, r, i: (which, b, r, jnp.maximum(i * bpt - 1, 0), 0))
    o, lse = pl.pallas_call(
        functools.partial(_dil_attn_kernel, tl=tl),
        out_shape=(
            jax.ShapeDtypeStruct((B, L, d * DIL_GROUP_W), BF16),
            jax.ShapeDtypeStruct((B, L, d * LANES), F32),
        ),
        grid=(B, d, L // tl),
        in_specs=[cur(0), cur(1), prev(1), cur(2), prev(2),
                  pl.BlockSpec(bias.shape, lambda b, r, i: (0, 0, 0))],
        out_specs=(
            pl.BlockSpec((1, tl, DIL_GROUP_W), lambda b, r, i: (b, i, r)),
            pl.BlockSpec((1, tl, LANES), lambda b, r, i: (b, i, r)),
        ),
        compiler_params=_params("parallel", "parallel", "arbitrary"),
        name=f"dil_attn_d{d}",
    )(qkv, qkv, qkv, qkv, qkv, bias)
    return o.reshape(B * L * d, DIL_GROUP_W), lse.reshape(B * L * d, LANES)


def _dil_out_kernel(x_ref, o0_ref, o1_ref, o2_ref, l0_ref, l1_ref, l2_ref, w_ref, out_ref, mrg_ref):
    lses = (l0_ref[...], l1_ref[...], l2_ref[...])
    mx = jnp.maximum(jnp.maximum(lses[0], lses[1]), lses[2])
    es = [jnp.exp(t - mx) for t in lses]
    den = es[0] + es[1] + es[2]
    ws = [e / den for e in es]
    o_refs = (o0_ref, o1_ref, o2_ref)
    for h in range(DIL_HEADS):
        cols = slice(h * DIL_HEAD_DIM, (h + 1) * DIL_HEAD_DIM)
        acc = None
        for g in range(DIL_GROUPS):
            term = ws[g][:, h:h + 1] * o_refs[g][:, cols].astype(F32)
            acc = term if acc is None else acc + term
        mrg_ref[:, cols] = acc.astype(BF16)
    out_ref[...] = x_ref[...] + _dot(mrg_ref[...], w_ref[...])


def _dil_out(x, os, lses, w, *, tm):
    T, D = x.shape
    row = lambda i: (i, 0)
    return pl.pallas_call(
        _dil_out_kernel,
        out_shape=jax.ShapeDtypeStruct((T, D), F32),
        grid=(T // tm,),
        in_specs=[pl.BlockSpec((tm, D), row)]
        + [pl.BlockSpec((tm, DIL_GROUP_W), row)] * DIL_GROUPS
        + [pl.BlockSpec((tm, LANES), row)] * DIL_GROUPS
        + [pl.BlockSpec(w.shape, lambda i: (0, 0))],
        out_specs=pl.BlockSpec((tm, D), row),
        scratch_shapes=[pltpu.VMEM((tm, DIL_GROUP_W), BF16)],
        compiler_params=_params("parallel"),
        name="dil_out",
    )(x, *os, *lses, w)


def _pad_rope(t):
    z = jnp.zeros(t.shape[:-1] + (LANES // 2 - ROPE_HALF,), t.dtype)
    return jnp.concatenate([t[..., :ROPE_HALF], z, t[..., ROPE_HALF:], z], axis=-1)


def _pad_qk(t):
    return jnp.concatenate([t[..., :NOPE_DIM], _pad_rope(t[..., NOPE_DIM:])], axis=-1)


def _rope_tables(S):
    inv = 1.0 / (ROPE_THETA ** (jnp.arange(0, ROPE_DIM, 2, dtype=F32) / ROPE_DIM))
    ang = jnp.arange(S, dtype=F32)[:, None] * inv[None, :]
    cos, sin = jnp.cos(ang), jnp.sin(ang)
    return _pad_rope(jnp.concatenate([cos, cos], -1)), _pad_rope(jnp.concatenate([-sin, sin], -1))


def _dil_bias(group, dilation):
    total = DIL_GROUPS * DIL_HEADS
    slopes = 2.0 ** (-8.0 * np.arange(1, total + 1, dtype=np.float32) / total)
    slopes = slopes.reshape(DIL_GROUPS, DIL_HEADS)[group]
    window = DIL_PAIRS[group][0] // dilation
    dist = np.arange(DIL_BLK)[:, None] + DIL_BLK - np.arange(2 * DIL_BLK)[None, :]
    ok = (dist >= 0) & (dist <= window)
    bias = -slopes[:, None, None] * (dilation * dist).astype(np.float32)[None]
    return jnp.asarray(np.where(ok[None], bias, np.float32(MASKED)), dtype=F32)


def _tile(n, want):
    t = min(n, want)
    assert n % t == 0, (n, want)
    return t


def kernel(x, ffn1_norm, ffn1_w_in, ffn1_w_out, mix_norm, ffn2_norm, ffn2_w_in, ffn2_w_out,
           mla_w_down, mla_g_cq, mla_g_ckv, mla_w_uq, mla_w_ukv, mla_g_qn, mla_g_kn, mla_w_o,
           dil_w_qkv, dil_g_qn, dil_g_kn, dil_w_o):
    B, S, D = x.shape
    T = B * S
    depth = ffn1_norm.shape[0]
    F = ffn1_w_out.shape[1]
    tm = _tile(T, 512)
    tf = _tile(F, 512)
    row = lambda v: v.reshape(1, -1).astype(F32)

    def ffn(xt, g, w_in, w_out):
        return _ffn(xt, row(g), w_in.astype(BF16), w_out.astype(BF16), tm=tm, tf=tf)

    xt = x.reshape(T, D)
    for i in range(depth):
        j = i // 2
        xt = ffn(xt, ffn1_norm[i], ffn1_w_in[i], ffn1_w_out[i])
        if i % 2 == 0:
            q_lora, kv_lora = mla_g_cq.shape[1], mla_g_ckv.shape[1]
            heads = mla_w_uq.shape[2] // QK_DIM
            wd = mla_w_down[j]
            wd = jnp.concatenate([wd[:, :q_lora + kv_lora], _pad_rope(wd[:, q_lora + kv_lora:])], -1)
            wuq = _pad_qk(mla_w_uq[j].reshape(q_lora, heads, QK_DIM)).reshape(q_lora, heads * QK_PAD)
            cos, sin = _rope_tables(S)
            q, k, v = _mla_proj(
                xt.reshape(B, S, D), row(mix_norm[i]), wd.astype(BF16), row(mla_g_cq[j]), row(mla_g_ckv[j]),
                wuq.astype(BF16), mla_w_ukv[j].astype(BF16),
                row(_pad_qk(mla_g_qn[j]) * (1.0 / math.sqrt(QK_DIM))), row(_pad_qk(mla_g_kn[j])),
                cos, sin, heads=heads, tm=_tile(S, 256))
            o = _mla_attn(q, k, v, tq=_tile(S, 1024), tk=512, sub=256)
            xt = _out_proj(xt, o.reshape(T, heads * V_DIM), mla_w_o[j].astype(BF16), tm=tm)
        else:
            gains = jnp.stack([dil_g_qn[j] * (1.0 / math.sqrt(DIL_HEAD_DIM)), dil_g_kn[j],
                               jnp.ones_like(dil_g_kn[j])]).reshape(3, 1, DIL_HEAD_DIM)
            qkvs = _dil_qkv(xt, row(mix_norm[i]), dil_w_qkv[j].astype(BF16), gains, batch=B, tm=_tile(S, 1024))
            os, lses = [], []
            for g, (_, dilation) in enumerate(DIL_PAIRS):
                o_g, lse_g = _dil_attn(qkvs[g], _dil_bias(g, dilation), tl=_tile(S // dilation, 512))
                os.append(o_g)
                lses.append(lse_g)
            xt = _dil_out(xt, os, lses, dil_w_o[j].astype(BF16), tm=tm)
        xt = ffn(xt, ffn2_norm[i], ffn2_w_in[i], ffn2_w_out[i])
    return xt.reshape(B, S, D)
```

```python
import functools
import math

import numpy as np
import jax
import jax.numpy as jnp
from jax import lax
from jax.experimental import pallas as pl
from jax.experimental.pallas import tpu as pltpu

EPS = 1e-6
ROPE_THETA = 10000.0
MASKED = -1e30
LOG2E = math.log2(math.e)

NOPE_DIM = 128
ROPE_DIM = 64
ROPE_HALF = ROPE_DIM // 2
V_DIM = 128
QK_DIM = NOPE_DIM + ROPE_DIM
QK_PAD = 256
DIL_PAIRS = ((128, 1), (512, 4), (2048, 16))
DIL_GROUPS = len(DIL_PAIRS)
DIL_HEADS = 8
DIL_HEAD_DIM = 128
DIL_BLK = 128
DIL_GROUP_W = DIL_HEADS * DIL_HEAD_DIM

LANES = 128
VMEM_LIMIT = 56 * 1024 * 1024

BF16 = jnp.bfloat16
F32 = jnp.float32


def _params(*semantics, flags=None):
    return pltpu.CompilerParams(dimension_semantics=semantics, vmem_limit_bytes=VMEM_LIMIT, flags=flags)


def _rms(t):
    return t * lax.rsqrt(jnp.mean(t * t, axis=-1, keepdims=True) + EPS)


def _dot(a, b):
    return jnp.dot(a, b, preferred_element_type=F32)


def _dot_nt(a, b):
    return lax.dot_general(a, b, (((1,), (1,)), ((), ())), preferred_element_type=F32)


def _ffn_kernel(x_hbm, g_ref, wg_ref, wu_ref, wo_ref, o_ref, x_buf, xn_ref, sem):
    i, j = pl.program_id(0), pl.program_id(1)
    tm = o_ref.shape[0]
    prefetch_step = min(1, pl.num_programs(1) - 1)

    def x_copy(tile):
        return pltpu.make_async_copy(x_hbm.at[pl.ds(tile * tm, tm), :], x_buf, sem)

    @pl.when(jnp.logical_and(i == 0, j == 0))
    def _():
        x_copy(0).start()

    @pl.when(j == 0)
    def _():
        x_copy(i).wait()
        x = x_buf[...]
        xn_ref[...] = (_rms(x) * g_ref[...]).astype(BF16)
        o_ref[...] = x

    @pl.when(jnp.logical_and(j == prefetch_step, i + 1 < pl.num_programs(0)))
    def _():
        x_copy(i + 1).start()

    xn = xn_ref[...]
    gate = _dot(xn, wg_ref[...])
    up = _dot(xn, wu_ref[...])
    h = (gate * jax.nn.sigmoid(gate) * up).astype(BF16)
    o_ref[...] += 0.5 * _dot(h, wo_ref[...])


def _ffn(x, g, w_in, w_out, *, tm, tf):
    T, D = x.shape
    F = w_out.shape[0]
    nf = F // tf
    return pl.pallas_call(
        _ffn_kernel,
        out_shape=jax.ShapeDtypeStruct((T, D), F32),
        grid=(T // tm, nf),
        in_specs=[
            pl.BlockSpec(memory_space=pl.ANY),
            pl.BlockSpec((1, D), lambda i, j: (0, 0)),
            pl.BlockSpec((D, tf), lambda i, j: (0, j)),
            pl.BlockSpec((D, tf), lambda i, j: (0, j + nf)),
            pl.BlockSpec((tf, D), lambda i, j: (j, 0)),
        ],
        out_specs=pl.BlockSpec((tm, D), lambda i, j: (i, 0)),
        scratch_shapes=[pltpu.VMEM((tm, D), F32), pltpu.VMEM((tm, D), BF16), pltpu.SemaphoreType.DMA],
        compiler_params=_params("arbitrary", "arbitrary"),
        name="ffn",
    )(x, g, w_in, w_in, w_out)


def _rope(t, cos_ref, sin_ref):
    return t * cos_ref[...] + pltpu.roll(t, LANES // 2, 1) * sin_ref[...]


def _mla_proj_kernel(x_ref, g_ref, wd_ref, gcq_ref, gckv_ref, wuq_ref, wukv_ref, gq_ref, gk_ref,
                     cos_ref, sin_ref, q_ref, k_ref, v_ref, *, heads, q_lora, kv_lora):
    xn = (_rms(x_ref[0]) * g_ref[...]).astype(BF16)
    lat = _dot(xn, wd_ref[...])
    cq = (_rms(lat[:, :q_lora]) * gcq_ref[...]).astype(BF16)
    ckv = (_rms(lat[:, q_lora:q_lora + kv_lora]) * gckv_ref[...]).astype(BF16)
    kpe = lat[:, q_lora + kv_lora:]
    kpe_ssq = jnp.sum(kpe * kpe, axis=-1, keepdims=True)
    kpe_rot = _rope(kpe * gk_ref[:, NOPE_DIM:], cos_ref, sin_ref)
    gq_n, gq_r = gq_ref[:, :NOPE_DIM], gq_ref[:, NOPE_DIM:]
    gk_n = gk_ref[:, :NOPE_DIM]
    for h in range(heads):
        cols = slice(h * QK_PAD, (h + 1) * QK_PAD)
        q = _dot(cq, wuq_ref[:, cols])
        qn, qr = q[:, :NOPE_DIM], q[:, NOPE_DIM:]
        ssq = jnp.sum(qn * qn + qr * qr, axis=-1, keepdims=True)
        rs = lax.rsqrt(ssq * (1.0 / QK_DIM) + EPS)
        q_ref[0, h, :, :NOPE_DIM] = (qn * rs * gq_n).astype(BF16)
        q_ref[0, h, :, NOPE_DIM:] = _rope(qr * rs * gq_r, cos_ref, sin_ref).astype(BF16)
        kv = _dot(ckv, wukv_ref[:, cols])
        kn = kv[:, :NOPE_DIM]
        ssk = jnp.sum(kn * kn, axis=-1, keepdims=True) + kpe_ssq
        rsk = lax.rsqrt(ssk * (1.0 / QK_DIM) + EPS)
        k_ref[0, h, :, :NOPE_DIM] = (kn * rsk * gk_n).astype(BF16)
        k_ref[0, h, :, NOPE_DIM:] = (kpe_rot * rsk).astype(BF16)
        v_ref[0, h, 0] = kv[:, NOPE_DIM:].T.astype(BF16)


def _mla_proj(x, g, wd, gcq, gckv, wuq, wukv, gq, gk, cos, sin, *, heads, tm, tk):
    B, S, D = x.shape
    q_lora, kv_lora = gcq.shape[1], gckv.shape[1]
    const = lambda b, i: (0, 0)
    kern = functools.partial(_mla_proj_kernel, heads=heads, q_lora=q_lora, kv_lora=kv_lora)
    return pl.pallas_call(
        kern,
        out_shape=(
            jax.ShapeDtypeStruct((B, heads, S, QK_PAD), BF16),
            jax.ShapeDtypeStruct((B, heads, S, QK_PAD), BF16),
            jax.ShapeDtypeStruct((B, heads, S // tk, V_DIM, tk), BF16),
        ),
        grid=(B, S // tm),
        in_specs=[
            pl.BlockSpec((1, tm, D), lambda b, i: (b, i, 0)),
            pl.BlockSpec(g.shape, const),
            pl.BlockSpec(wd.shape, const),
            pl.BlockSpec(gcq.shape, const),
            pl.BlockSpec(gckv.shape, const),
            pl.BlockSpec(wuq.shape, const),
            pl.BlockSpec(wukv.shape, const),
            pl.BlockSpec(gq.shape, const),
            pl.BlockSpec(gk.shape, const),
            pl.BlockSpec((tm, LANES), lambda b, i: (i, 0)),
            pl.BlockSpec((tm, LANES), lambda b, i: (i, 0)),
        ],
        out_specs=(
            pl.BlockSpec((1, heads, tm, QK_PAD), lambda b, i: (b, 0, i, 0)),
            pl.BlockSpec((1, heads, tm, QK_PAD), lambda b, i: (b, 0, i, 0)),
            pl.BlockSpec((1, heads, 1, V_DIM, tm), lambda b, i: (b, 0, i // (tk // tm), 0, i % (tk // tm))),
        ),
        compiler_params=_params("parallel", "parallel"),
        name="mla_proj",
    )(x, g, wd, gcq, gckv, wuq, wukv, gq, gk, cos, sin)


def _mla_attn_kernel(q_ref, k_ref, vt_ref, o_ref, m_ref, l_ref, acc_ref, *, tq, tk, sub):
    qi = pl.program_id(2)
    m_ref[...] = jnp.full(m_ref.shape, MASKED, F32)
    l_ref[...] = jnp.zeros(l_ref.shape, F32)
    acc_ref[...] = jnp.zeros(acc_ref.shape, F32)

    def scores(sb, k, causal):
        st = _dot_nt(k, q_ref[0, 0, sb * sub:(sb + 1) * sub, :])
        if causal:
            key = lax.broadcasted_iota(jnp.int32, st.shape, 0)
            qry = lax.broadcasted_iota(jnp.int32, st.shape, 1) + sb * sub
            st = jnp.where(key <= qry, st, MASKED)
        return st

    def softmax(sb, st):
        qs = slice(sb * sub, (sb + 1) * sub)
        m_old = m_ref[:, qs]
        m_new = jnp.maximum(m_old, jnp.max(st, axis=0, keepdims=True))
        alpha = jnp.exp2(m_old - m_new)
        p = jnp.exp2(st - m_new)
        l_ref[:, qs] = alpha * l_ref[:, qs] + jnp.sum(p, axis=0, keepdims=True)
        m_ref[:, qs] = m_new
        return alpha, p.astype(BF16)

    def accumulate(sb, vt, alpha, p):
        qs = slice(sb * sub, (sb + 1) * sub)
        acc_ref[:, qs] = alpha * acc_ref[:, qs] + _dot(vt, p)

    def staged(ks, vts, causal):
        n = tq // sub
        sts = [scores(sb, ks[sb], causal) for sb in range(n)]
        for sb in range(n):
            accumulate(sb, vts[sb], *softmax(sb, sts[sb]))

    def body(kc, carry):
        k = k_ref[0, 0, pl.ds(pl.multiple_of(kc * tk, tk), tk), :]
        vt = vt_ref[0, 0, kc]
        staged([k] * (tq // sub), [vt] * (tq // sub), False)
        return carry

    chunks = qi * (tq // tk)
    lax.fori_loop(0, chunks, body, 0)
    widths = [(sb + 1) * sub for sb in range(tq // sub)]
    ks = [k_ref[0, 0, pl.ds(pl.multiple_of(qi * tq, tq), w), :] for w in widths]
    vts = [jnp.concatenate([vt_ref[0, 0, chunks + c, :, :min(tk, w - c * tk)] for c in range(pl.cdiv(w, tk))], axis=1)
           for w in widths]
    staged(ks, vts, True)
    o_ref[0] = (acc_ref[...] / l_ref[...]).T.astype(o_ref.dtype)


def _mla_attn(q, k, vt, *, tq, sub):
    B, H, S, _ = q.shape
    nc, _, tk = vt.shape[2:]
    return pl.pallas_call(
        functools.partial(_mla_attn_kernel, tq=tq, tk=tk, sub=sub),
        out_shape=jax.ShapeDtypeStruct((B, S, H * V_DIM), BF16),
        grid=(B, H, S // tq),
        in_specs=[
            pl.BlockSpec((1, 1, tq, QK_PAD), lambda b, h, i: (b, h, i, 0)),
            pl.BlockSpec((1, 1, S, QK_PAD), lambda b, h, i: (b, h, 0, 0)),
            pl.BlockSpec((1, 1, nc, V_DIM, tk), lambda b, h, i: (b, h, 0, 0, 0)),
        ],
        out_specs=pl.BlockSpec((1, tq, V_DIM), lambda b, h, i: (b, i, h)),
        scratch_shapes=[
            pltpu.VMEM((1, tq), F32),
            pltpu.VMEM((1, tq), F32),
            pltpu.VMEM((V_DIM, tq), F32),
        ],
        compiler_params=_params("parallel", "parallel", "arbitrary"),
        name="mla_attn",
    )(q, k, vt)


def _out_proj_kernel(x_ref, o_ref, w_ref, out_ref):
    out_ref[...] = x_ref[...] + _dot(o_ref[...], w_ref[...])


def _out_proj(x, o, w, *, tm):
    T, D = x.shape
    return pl.pallas_call(
        _out_proj_kernel,
        out_shape=jax.ShapeDtypeStruct((T, D), F32),
        grid=(T // tm,),
        in_specs=[
            pl.BlockSpec((tm, D), lambda i: (i, 0)),
            pl.BlockSpec((tm, o.shape[1]), lambda i: (i, 0)),
            pl.BlockSpec(w.shape, lambda i: (0, 0)),
        ],
        out_specs=pl.BlockSpec((tm, D), lambda i: (i, 0)),
        compiler_params=_params("parallel"),
        name="out_proj",
    )(x, o, w)


def _dil_qkv_kernel(x_ref, g_ref, w_ref, gain_ref, o0_ref, o1_ref, o2_ref, xn_ref, y_ref):
    j = pl.program_id(1)
    tm = x_ref.shape[0]

    @pl.when(j == 0)
    def _():
        xn_ref[...] = (_rms(x_ref[...]) * g_ref[...]).astype(BF16)

    is_v = j % 3 == 2
    pair_w = 2 * DIL_HEAD_DIM

    for group, (o_ref, (_, d)) in enumerate(zip((o0_ref, o1_ref, o2_ref), DIL_PAIRS)):
        @pl.when(j // 3 == group)
        def _(o_ref=o_ref, d=d):
            xn = xn_ref[...]
            ys = [_dot(xn, w_ref[:, c * pair_w:(c + 1) * pair_w]) for c in range(DIL_HEADS // 2)]
            for h in range(DIL_HEADS):
                cols = slice(h * DIL_HEAD_DIM, (h + 1) * DIL_HEAD_DIM)
                y = ys[h // 2][:, (h % 2) * DIL_HEAD_DIM:(h % 2 + 1) * DIL_HEAD_DIM]
                rs = lax.rsqrt(jnp.mean(y * y, axis=-1, keepdims=True) + EPS)
                z = y * jnp.where(is_v, 1.0, rs) * gain_ref[0]
                if d == 1:
                    o_ref[0, 0, 0, :, cols] = z.astype(BF16)
                else:
                    y_ref[h] = z
                    for r in range(d):
                        o_ref[0, 0, r, :, cols] = y_ref[h, pl.ds(r, tm // d, stride=d), :].astype(BF16)


def _dil_qkv(x, g, w, gains, *, batch, tm):
    T, D = x.shape
    S = T // batch
    tiles = S // tm
    wcol = lambda i, j: (0, (j % 3) * DIL_GROUPS + j // 3)
    out_shapes, out_specs = [], []
    for group, (_, d) in enumerate(DIL_PAIRS):
        out_shapes.append(jax.ShapeDtypeStruct((3, batch, d, S // d, DIL_GROUP_W), BF16))
        out_specs.append(pl.BlockSpec(
            (1, 1, d, tm // d, DIL_GROUP_W),
            lambda i, j, group=group: (jnp.clip(j - 3 * group, 0, 2), i // tiles, 0, i % tiles, 0)))
    return pl.pallas_call(
        _dil_qkv_kernel,
        out_shape=tuple(out_shapes),
        grid=(T // tm, 3 * DIL_GROUPS),
        in_specs=[
            pl.BlockSpec((tm, D), lambda i, j: (i, 0)),
            pl.BlockSpec((1, D), lambda i, j: (0, 0)),
            pl.BlockSpec((D, DIL_GROUP_W), wcol),
            pl.BlockSpec((1, 1, DIL_HEAD_DIM), lambda i, j: (j % 3, 0, 0)),
        ],
        out_specs=tuple(out_specs),
        scratch_shapes=[pltpu.VMEM((tm, D), BF16), pltpu.VMEM((DIL_HEADS, tm, DIL_HEAD_DIM), F32)],
        compiler_params=_params("arbitrary", "arbitrary"),
        name="dil_qkv",
    )(x, g, w, gains)


def _dil_attn_kernel(q_ref, k_ref, kp_ref, v_ref, vp_ref, bias_ref, o_ref, lse_ref, *, tl):
    i = pl.program_id(2)
    first_pen = jnp.where(i == 0, MASKED, 0.0).astype(F32)
    lane = lax.broadcasted_iota(jnp.int32, (DIL_BLK, LANES), 1)
    q_ref, k_ref, kp_ref, v_ref, vp_ref = (t.at[0, 0, 0] for t in (q_ref, k_ref, kp_ref, v_ref, vp_ref))
    for qb in range(tl // DIL_BLK):
        rows = slice(qb * DIL_BLK, (qb + 1) * DIL_BLK)
        lse_tile = jnp.zeros((DIL_BLK, LANES), F32)
        for h in range(DIL_HEADS):
            cols = slice(h * DIL_HEAD_DIM, (h + 1) * DIL_HEAD_DIM)
            q = q_ref[rows, cols]
            if qb == 0:
                s_prev = _dot_nt(q, kp_ref[:, cols]) + first_pen
                s = jnp.concatenate([s_prev, _dot_nt(q, k_ref[rows, cols])], axis=1)
            else:
                s = _dot_nt(q, k_ref[(qb - 1) * DIL_BLK:(qb + 1) * DIL_BLK, cols])
            s = s + bias_ref[h]
            m = jnp.max(s, axis=-1, keepdims=True)
            p = jnp.exp(s - m)
            l = jnp.sum(p, axis=-1, keepdims=True)
            pb = p.astype(BF16)
            if qb == 0:
                o = _dot(pb[:, :DIL_BLK], vp_ref[:, cols]) + _dot(pb[:, DIL_BLK:], v_ref[rows, cols])
            else:
                o = _dot(pb, v_ref[(qb - 1) * DIL_BLK:(qb + 1) * DIL_BLK, cols])
            o_ref[0, 0, rows, cols] = (o / l).astype(o_ref.dtype)
            lse_tile = jnp.where(lane == h, m + jnp.log(l), lse_tile)
        lse_ref[0, 0, rows, :] = lse_tile


def _dil_attn(qkv, bias, *, tl):
    _, B, d, L, _ = qkv.shape
    bpt = tl // DIL_BLK
    cur = lambda which: pl.BlockSpec((1, 1, 1, tl, DIL_GROUP_W), lambda b, r, i: (which, b, r, i, 0))
    prev = lambda which: pl.BlockSpec((1, 1, 1, DIL_BLK, DIL_GROUP_W),
                                      lambda b, r, i: (which, b, r, jnp.maximum(i * bpt - 1, 0), 0))
    return pl.pallas_call(
        functools.partial(_dil_attn_kernel, tl=tl),
        out_shape=(
            jax.ShapeDtypeStruct((B, d, L, DIL_GROUP_W), BF16),
            jax.ShapeDtypeStruct((B, d, L, LANES), F32),
        ),
        grid=(B, d, L // tl),
        in_specs=[cur(0), cur(1), prev(1), cur(2), prev(2),
                  pl.BlockSpec(bias.shape, lambda b, r, i: (0, 0, 0))],
        out_specs=(
            pl.BlockSpec((1, 1, tl, DIL_GROUP_W), lambda b, r, i: (b, r, i, 0)),
            pl.BlockSpec((1, 1, tl, LANES), lambda b, r, i: (b, r, i, 0)),
        ),
        compiler_params=_params("parallel", "parallel", "arbitrary"),
        name=f"dil_attn_d{d}",
    )(qkv, qkv, qkv, qkv, qkv, bias)


def _dil_out_kernel(x_ref, o0_ref, o1_ref, o2_ref, l0_ref, l1_ref, l2_ref, w_ref, out_ref,
                    u_ref, lt_ref, mrg_ref):
    tm = x_ref.shape[0]
    for gi, (o_ref, l_ref, (_, d)) in enumerate(zip((o1_ref, o2_ref), (l1_ref, l2_ref), DIL_PAIRS[1:])):
        for r in range(d):
            rows = pl.ds(r, tm // d, stride=d)
            lt_ref[gi, rows, :] = l_ref[0, r]
            for h in range(DIL_HEADS):
                u_ref[gi, h, rows, :] = o_ref[0, r, :, h * DIL_HEAD_DIM:(h + 1) * DIL_HEAD_DIM].astype(F32)
    lses = (l0_ref[0, 0], lt_ref[0], lt_ref[1])
    mx = jnp.maximum(jnp.maximum(lses[0], lses[1]), lses[2])
    es = [jnp.exp(t - mx) for t in lses]
    den = es[0] + es[1] + es[2]
    ws = [e / den for e in es]
    for h in range(DIL_HEADS):
        cols = slice(h * DIL_HEAD_DIM, (h + 1) * DIL_HEAD_DIM)
        acc = ws[0][:, h:h + 1] * o0_ref[0, 0, :, cols].astype(F32)
        for gi in range(DIL_GROUPS - 1):
            acc = acc + ws[gi + 1][:, h:h + 1] * u_ref[gi, h]
        mrg_ref[:, cols] = acc.astype(BF16)
    out_ref[...] = x_ref[...] + _dot(mrg_ref[...], w_ref[...])


def _dil_out(x, os, lses, w, *, batch, tm):
    T, D = x.shape
    tiles = T // batch // tm
    row = lambda i: (i, 0)
    split = lambda d, width: pl.BlockSpec((1, d, tm // d, width), lambda i: (i // tiles, 0, i % tiles, 0))
    return pl.pallas_call(
        _dil_out_kernel,
        out_shape=jax.ShapeDtypeStruct((T, D), F32),
        grid=(T // tm,),
        in_specs=[pl.BlockSpec((tm, D), row)]
        + [split(d, DIL_GROUP_W) for _, d in DIL_PAIRS]
        + [split(d, LANES) for _, d in DIL_PAIRS]
        + [pl.BlockSpec(w.shape, lambda i: (0, 0))],
        out_specs=pl.BlockSpec((tm, D), row),
        scratch_shapes=[
            pltpu.VMEM((DIL_GROUPS - 1, DIL_HEADS, tm, DIL_HEAD_DIM), F32),
            pltpu.VMEM((DIL_GROUPS - 1, tm, LANES), F32),
            pltpu.VMEM((tm, DIL_GROUP_W), BF16),
        ],
        compiler_params=_params("parallel"),
        name="dil_out",
    )(x, *os, *lses, w)


def _pad_rope(t):
    z = jnp.zeros(t.shape[:-1] + (LANES // 2 - ROPE_HALF,), t.dtype)
    return jnp.concatenate([t[..., :ROPE_HALF], z, t[..., ROPE_HALF:], z], axis=-1)


def _pad_qk(t):
    return jnp.concatenate([t[..., :NOPE_DIM], _pad_rope(t[..., NOPE_DIM:])], axis=-1)


def _rope_tables(S):
    inv = 1.0 / (ROPE_THETA ** (jnp.arange(0, ROPE_DIM, 2, dtype=F32) / ROPE_DIM))
    ang = jnp.arange(S, dtype=F32)[:, None] * inv[None, :]
    cos, sin = jnp.cos(ang), jnp.sin(ang)
    return _pad_rope(jnp.concatenate([cos, cos], -1)), _pad_rope(jnp.concatenate([-sin, sin], -1))


def _dil_bias(group, dilation):
    total = DIL_GROUPS * DIL_HEADS
    slopes = 2.0 ** (-8.0 * np.arange(1, total + 1, dtype=np.float32) / total)
    slopes = slopes.reshape(DIL_GROUPS, DIL_HEADS)[group]
    window = DIL_PAIRS[group][0] // dilation
    dist = np.arange(DIL_BLK)[:, None] + DIL_BLK - np.arange(2 * DIL_BLK)[None, :]
    ok = (dist >= 0) & (dist <= window)
    bias = -slopes[:, None, None] * (dilation * dist).astype(np.float32)[None]
    return jnp.asarray(np.where(ok[None], bias, np.float32(MASKED)), dtype=F32)


def _tile(n, want):
    t = min(n, want)
    assert n % t == 0, (n, want)
    return t


def kernel(x, ffn1_norm, ffn1_w_in, ffn1_w_out, mix_norm, ffn2_norm, ffn2_w_in, ffn2_w_out,
           mla_w_down, mla_g_cq, mla_g_ckv, mla_w_uq, mla_w_ukv, mla_g_qn, mla_g_kn, mla_w_o,
           dil_w_qkv, dil_g_qn, dil_g_kn, dil_w_o):
    B, S, D = x.shape
    T = B * S
    depth = ffn1_norm.shape[0]
    F = ffn1_w_out.shape[1]
    tm = _tile(T, 512)
    tf = _tile(F, 512)
    row = lambda v: v.reshape(1, -1).astype(F32)

    def ffn(xt, g, w_in, w_out):
        return _ffn(xt, row(g), w_in.astype(BF16), w_out.astype(BF16), tm=_tile(T, 1024), tf=tf)

    xt = x.reshape(T, D)
    for i in range(depth):
        j = i // 2
        xt = ffn(xt, ffn1_norm[i], ffn1_w_in[i], ffn1_w_out[i])
        if i % 2 == 0:
            q_lora, kv_lora = mla_g_cq.shape[1], mla_g_ckv.shape[1]
            heads = mla_w_uq.shape[2] // QK_DIM
            wd = mla_w_down[j]
            wd = jnp.concatenate([wd[:, :q_lora + kv_lora], _pad_rope(wd[:, q_lora + kv_lora:])], -1)
            wuq = _pad_qk(mla_w_uq[j].reshape(q_lora, heads, QK_DIM)).reshape(q_lora, heads * QK_PAD)
            cos, sin = _rope_tables(S)
            q, k, vt = _mla_proj(
                xt.reshape(B, S, D), row(mix_norm[i]), wd.astype(BF16), row(mla_g_cq[j]), row(mla_g_ckv[j]),
                wuq.astype(BF16), mla_w_ukv[j].astype(BF16),
                row(_pad_qk(mla_g_qn[j]) * (LOG2E / math.sqrt(QK_DIM))), row(_pad_qk(mla_g_kn[j])),
                cos, sin, heads=heads, tm=_tile(S, 256), tk=_tile(S, 1024))
            o = _mla_attn(q, k, vt, tq=_tile(S, 1024), sub=256)
            xt = _out_proj(xt, o.reshape(T, heads * V_DIM), mla_w_o[j].astype(BF16), tm=tm)
        else:
            gains = jnp.stack([dil_g_qn[j] * (1.0 / math.sqrt(DIL_HEAD_DIM)), dil_g_kn[j],
                               jnp.ones_like(dil_g_kn[j])]).reshape(3, 1, DIL_HEAD_DIM)
            qkvs = _dil_qkv(xt, row(mix_norm[i]), dil_w_qkv[j].astype(BF16), gains, batch=B, tm=_tile(S, 1024))
            os, lses = [], []
            for g, (_, dilation) in enumerate(DIL_PAIRS):
                o_g, lse_g = _dil_attn(qkvs[g], _dil_bias(g, dilation), tl=_tile(S // dilation, 512))
                os.append(o_g)
                lses.append(lse_g)
            xt = _dil_out(xt, os, lses, dil_w_o[j].astype(BF16), batch=B, tm=_tile(S, 512))
        xt = ffn(xt, ffn2_norm[i], ffn2_w_in[i], ffn2_w_out[i])
    return xt.reshape(B, S, D)
```

```python
import functools
import math

import numpy as np
import jax
import jax.numpy as jnp
from jax import lax
from jax.experimental import pallas as pl
from jax.experimental.pallas import tpu as pltpu

EPS = 1e-6
ROPE_THETA = 10000.0
MASKED = -1e30
LOG2E = math.log2(math.e)

NOPE_DIM = 128
ROPE_DIM = 64
ROPE_HALF = ROPE_DIM // 2
V_DIM = 128
QK_DIM = NOPE_DIM + ROPE_DIM
QK_PAD = 256
DIL_PAIRS = ((128, 1), (512, 4), (2048, 16))
DIL_GROUPS = len(DIL_PAIRS)
DIL_HEADS = 8
DIL_HEAD_DIM = 128
DIL_BLK = 128
DIL_GROUP_W = DIL_HEADS * DIL_HEAD_DIM

LANES = 128
VMEM_LIMIT = 56 * 1024 * 1024

BF16 = jnp.bfloat16
F32 = jnp.float32


def _params(*semantics, flags=None):
    return pltpu.CompilerParams(dimension_semantics=semantics, vmem_limit_bytes=VMEM_LIMIT, flags=flags)


def _rms(t):
    return t * lax.rsqrt(jnp.mean(t * t, axis=-1, keepdims=True) + EPS)


def _dot(a, b):
    return jnp.dot(a, b, preferred_element_type=F32)


def _dot_nt(a, b):
    return lax.dot_general(a, b, (((1,), (1,)), ((), ())), preferred_element_type=F32)


def _ffn_kernel(x_hbm, g_ref, wg_ref, wu_ref, wo_ref, *rest, cast_next):
    if cast_next:
        nwi_ref, nwo_ref, o_ref, cwi_ref, cwo_ref, x_buf, xn_ref, sem = rest
        cwi_ref[...] = nwi_ref[0].astype(BF16)
        cwo_ref[...] = nwo_ref[0].astype(BF16)
    else:
        o_ref, x_buf, xn_ref, sem = rest
    i, j = pl.program_id(0), pl.program_id(1)
    tm = o_ref.shape[0]
    prefetch_step = min(1, pl.num_programs(1) - 1)

    def x_copy(tile):
        return pltpu.make_async_copy(x_hbm.at[pl.ds(tile * tm, tm), :], x_buf, sem)

    @pl.when(jnp.logical_and(i == 0, j == 0))
    def _():
        x_copy(0).start()

    @pl.when(j == 0)
    def _():
        x_copy(i).wait()
        x = x_buf[...]
        xn_ref[...] = (_rms(x) * g_ref[...]).astype(BF16)
        o_ref[...] = x

    @pl.when(jnp.logical_and(j == prefetch_step, i + 1 < pl.num_programs(0)))
    def _():
        x_copy(i + 1).start()

    xn = xn_ref[...]
    gate = _dot(xn, wg_ref[...])
    up = _dot(xn, wu_ref[...])
    h = (gate * jax.nn.sigmoid(gate) * up).astype(BF16)
    o_ref[...] += 0.5 * _dot(h, wo_ref[...])


def _ffn(x, g, w_in, w_out, next_weights=None, *, tm, tf):
    T, D = x.shape
    F = w_out.shape[0]
    nt, nf = T // tm, F // tf
    in_specs = [
        pl.BlockSpec(memory_space=pl.ANY),
        pl.BlockSpec((1, D), lambda i, j: (0, 0)),
        pl.BlockSpec((D, tf), lambda i, j: (0, j)),
        pl.BlockSpec((D, tf), lambda i, j: (0, j + nf)),
        pl.BlockSpec((tf, D), lambda i, j: (j, 0)),
    ]
    out_shape = [jax.ShapeDtypeStruct((T, D), F32)]
    out_specs = [pl.BlockSpec((tm, D), lambda i, j: (i, 0))]
    operands = [x, g, w_in, w_in, w_out]
    if next_weights is not None:
        nwi, nwo, layer = next_weights
        in_specs += [pl.BlockSpec((1, D // nt, 2 * F // nf), lambda i, j: (layer, i, j)),
                     pl.BlockSpec((1, F // nf, D // nt), lambda i, j: (layer, j, i))]
        out_shape += [jax.ShapeDtypeStruct(nwi.shape[1:], BF16), jax.ShapeDtypeStruct(nwo.shape[1:], BF16)]
        out_specs += [pl.BlockSpec((D // nt, 2 * F // nf), lambda i, j: (i, j)),
                      pl.BlockSpec((F // nf, D // nt), lambda i, j: (j, i))]
        operands += [nwi, nwo]
    outs = pl.pallas_call(
        functools.partial(_ffn_kernel, cast_next=next_weights is not None),
        out_shape=tuple(out_shape),
        grid=(nt, nf),
        in_specs=in_specs,
        out_specs=tuple(out_specs),
        scratch_shapes=[pltpu.VMEM((tm, D), F32), pltpu.VMEM((tm, D), BF16), pltpu.SemaphoreType.DMA],
        compiler_params=_params("arbitrary", "arbitrary"),
        name="ffn",
    )(*operands)
    return outs[0], tuple(outs[1:])


def _rope(t, cos, sin):
    return t * cos + pltpu.roll(t, LANES // 2, 1) * sin


def _mla_proj_kernel(x_ref, g_ref, wd_ref, gcq_ref, gckv_ref, wuq_ref, wukv_ref, gq_ref, gk_ref,
                     cos_ref, sin_ref, q_ref, k_ref, v_ref, *, heads, q_lora, kv_lora, rows):
    chains = [slice(c * rows, (c + 1) * rows) for c in range(x_ref.shape[1] // rows)]
    gq_n, gq_r = gq_ref[:, :NOPE_DIM], gq_ref[:, NOPE_DIM:]
    gk_n, gk_r = gk_ref[:, :NOPE_DIM], gk_ref[:, NOPE_DIM:]
    xns = [(_rms(x_ref[0, rs, :]) * g_ref[...]).astype(BF16) for rs in chains]
    lats = [_dot(xn, wd_ref[...]) for xn in xns]
    cqs = [(_rms(lat[:, :q_lora]) * gcq_ref[...]).astype(BF16) for lat in lats]
    ckvs = [(_rms(lat[:, q_lora:q_lora + kv_lora]) * gckv_ref[...]).astype(BF16) for lat in lats]
    qs = [[_dot(cq, wuq_ref[:, h * QK_PAD:(h + 1) * QK_PAD]) for h in range(heads)] for cq in cqs]
    kvs = [[_dot(ckv, wukv_ref[:, h * QK_PAD:(h + 1) * QK_PAD]) for h in range(heads)] for ckv in ckvs]
    for c, rs in enumerate(chains):
        cos, sin = cos_ref[rs, :], sin_ref[rs, :]
        kpe = lats[c][:, q_lora + kv_lora:]
        kpe_ssq = jnp.sum(kpe * kpe, axis=-1, keepdims=True)
        kpe_rot = _rope(kpe * gk_r, cos, sin)
        for h in range(heads):
            q = qs[c][h]
            qn, qr = q[:, :NOPE_DIM], q[:, NOPE_DIM:]
            ssq = jnp.sum(qn * qn + qr * qr, axis=-1, keepdims=True)
            rsq = lax.rsqrt(ssq * (1.0 / QK_DIM) + EPS)
            q_ref[0, h, rs, :NOPE_DIM] = (qn * rsq * gq_n).astype(BF16)
            q_ref[0, h, rs, NOPE_DIM:] = _rope(qr * rsq * gq_r, cos, sin).astype(BF16)
            kv = kvs[c][h]
            kn = kv[:, :NOPE_DIM]
            ssk = jnp.sum(kn * kn, axis=-1, keepdims=True) + kpe_ssq
            rsk = lax.rsqrt(ssk * (1.0 / QK_DIM) + EPS)
            k_ref[0, h, rs, :NOPE_DIM] = (kn * rsk * gk_n).astype(BF16)
            k_ref[0, h, rs, NOPE_DIM:] = (kpe_rot * rsk).astype(BF16)
            v_ref[0, h, 0, :, rs] = kv[:, NOPE_DIM:].T.astype(BF16)


def _mla_proj(x, g, wd, gcq, gckv, wuq, wukv, gq, gk, cos, sin, *, heads, tm, rows, tk):
    B, S, D = x.shape
    q_lora, kv_lora = gcq.shape[1], gckv.shape[1]
    const = lambda b, i: (0, 0)
    once = pl.Buffered(1)
    kern = functools.partial(_mla_proj_kernel, heads=heads, q_lora=q_lora, kv_lora=kv_lora, rows=rows)
    return pl.pallas_call(
        kern,
        out_shape=(
            jax.ShapeDtypeStruct((B, heads, S, QK_PAD), BF16),
            jax.ShapeDtypeStruct((B, heads, S, QK_PAD), BF16),
            jax.ShapeDtypeStruct((B, heads, S // tk, V_DIM, tk), BF16),
        ),
        grid=(B, S // tm),
        in_specs=[
            pl.BlockSpec((1, tm, D), lambda b, i: (b, i, 0)),
            pl.BlockSpec(g.shape, const),
            pl.BlockSpec(wd.shape, const, pipeline_mode=once),
            pl.BlockSpec(gcq.shape, const),
            pl.BlockSpec(gckv.shape, const),
            pl.BlockSpec(wuq.shape, const, pipeline_mode=once),
            pl.BlockSpec(wukv.shape, const, pipeline_mode=once),
            pl.BlockSpec(gq.shape, const),
            pl.BlockSpec(gk.shape, const),
            pl.BlockSpec((tm, LANES), lambda b, i: (i, 0)),
            pl.BlockSpec((tm, LANES), lambda b, i: (i, 0)),
        ],
        out_specs=(
            pl.BlockSpec((1, heads, tm, QK_PAD), lambda b, i: (b, 0, i, 0)),
            pl.BlockSpec((1, heads, tm, QK_PAD), lambda b, i: (b, 0, i, 0)),
            pl.BlockSpec((1, heads, 1, V_DIM, tm), lambda b, i: (b, 0, i // (tk // tm), 0, i % (tk // tm))),
        ),
        compiler_params=_params("parallel", "parallel"),
        name="mla_proj",
    )(x, g, wd, gcq, gckv, wuq, wukv, gq, gk, cos, sin)


def _mla_attn_kernel(q_ref, k_ref, vt_ref, o_ref, m_ref, l_ref, acc_ref, *, tq, tk, sub):
    qi = pl.program_id(2)
    m_ref[...] = jnp.full(m_ref.shape, MASKED, F32)
    l_ref[...] = jnp.zeros(l_ref.shape, F32)
    acc_ref[...] = jnp.zeros(acc_ref.shape, F32)

    def scores(sb, k, causal):
        st = _dot_nt(k, q_ref[0, 0, sb * sub:(sb + 1) * sub, :])
        if causal:
            key = lax.broadcasted_iota(jnp.int32, st.shape, 0)
            qry = lax.broadcasted_iota(jnp.int32, st.shape, 1) + sb * sub
            st = jnp.where(key <= qry, st, MASKED)
        return st

    def softmax(sb, st):
        qs = slice(sb * sub, (sb + 1) * sub)
        m_old = m_ref[:, qs]
        m_new = jnp.maximum(m_old, jnp.max(st, axis=0, keepdims=True))
        alpha = jnp.exp2(m_old - m_new)
        p = jnp.exp2(st - m_new)
        l_ref[:, qs] = alpha * l_ref[:, qs] + jnp.sum(p, axis=0, keepdims=True)
        m_ref[:, qs] = m_new
        return alpha, p.astype(BF16)

    def accumulate(sb, vt, alpha, p):
        qs = slice(sb * sub, (sb + 1) * sub)
        acc_ref[:, qs] = alpha * acc_ref[:, qs] + _dot(vt, p)

    def staged(ks, vts, causal):
        n = tq // sub
        sts = [scores(sb, ks[sb], causal) for sb in range(n)]
        for sb in range(n):
            accumulate(sb, vts[sb], *softmax(sb, sts[sb]))

    def body(kc, carry):
        k = k_ref[0, 0, pl.ds(pl.multiple_of(kc * tk, tk), tk), :]
        vt = vt_ref[0, 0, kc]
        staged([k] * (tq // sub), [vt] * (tq // sub), False)
        return carry

    chunks = qi * (tq // tk)
    lax.fori_loop(0, chunks, body, 0)
    widths = [(sb + 1) * sub for sb in range(tq // sub)]
    ks = [k_ref[0, 0, pl.ds(pl.multiple_of(qi * tq, tq), w), :] for w in widths]
    vts = [jnp.concatenate([vt_ref[0, 0, chunks + c, :, :min(tk, w - c * tk)] for c in range(pl.cdiv(w, tk))], axis=1)
           for w in widths]
    staged(ks, vts, True)
    o_ref[0] = (acc_ref[...] / l_ref[...]).T.astype(o_ref.dtype)


def _mla_attn(q, k, vt, *, tq, sub):
    B, H, S, _ = q.shape
    nc, _, tk = vt.shape[2:]
    return pl.pallas_call(
        functools.partial(_mla_attn_kernel, tq=tq, tk=tk, sub=sub),
        out_shape=jax.ShapeDtypeStruct((B, S, H * V_DIM), BF16),
        grid=(B, H, S // tq),
        in_specs=[
            pl.BlockSpec((1, 1, tq, QK_PAD), lambda b, h, i: (b, h, i, 0)),
            pl.BlockSpec((1, 1, S, QK_PAD), lambda b, h, i: (b, h, 0, 0)),
            pl.BlockSpec((1, 1, nc, V_DIM, tk), lambda b, h, i: (b, h, 0, 0, 0)),
        ],
        out_specs=pl.BlockSpec((1, tq, V_DIM), lambda b, h, i: (b, i, h)),
        scratch_shapes=[
            pltpu.VMEM((1, tq), F32),
            pltpu.VMEM((1, tq), F32),
            pltpu.VMEM((V_DIM, tq), F32),
        ],
        compiler_params=_params("parallel", "parallel", "arbitrary"),
        name="mla_attn",
    )(q, k, vt)


def _out_proj_kernel(x_ref, o_ref, w_ref, out_ref):
    out_ref[...] = x_ref[...] + _dot(o_ref[...], w_ref[...])


def _out_proj(x, o, w, *, tm):
    T, D = x.shape
    return pl.pallas_call(
        _out_proj_kernel,
        out_shape=jax.ShapeDtypeStruct((T, D), F32),
        grid=(T // tm,),
        in_specs=[
            pl.BlockSpec((tm, D), lambda i: (i, 0)),
            pl.BlockSpec((tm, o.shape[1]), lambda i: (i, 0)),
            pl.BlockSpec(w.shape, lambda i: (0, 0)),
        ],
        out_specs=pl.BlockSpec((tm, D), lambda i: (i, 0)),
        compiler_params=_params("parallel"),
        name="out_proj",
    )(x, o, w)


def _dil_qkv_kernel(x_ref, g_ref, w_ref, gain_ref, o0_ref, o1_ref, o2_ref, xn_ref, y_ref):
    j = pl.program_id(1)
    tm = x_ref.shape[0]

    @pl.when(j == 0)
    def _():
        xn_ref[...] = (_rms(x_ref[...]) * g_ref[...]).astype(BF16)

    is_v = j % 3 == 2
    pair_w = 2 * DIL_HEAD_DIM

    for group, (o_ref, (_, d)) in enumerate(zip((o0_ref, o1_ref, o2_ref), DIL_PAIRS)):
        @pl.when(j // 3 == group)
        def _(o_ref=o_ref, d=d):
            xn = xn_ref[...]
            ys = [_dot(xn, w_ref[:, c * pair_w:(c + 1) * pair_w]) for c in range(DIL_HEADS // 2)]
            for h in range(DIL_HEADS):
                cols = slice(h * DIL_HEAD_DIM, (h + 1) * DIL_HEAD_DIM)
                y = ys[h // 2][:, (h % 2) * DIL_HEAD_DIM:(h % 2 + 1) * DIL_HEAD_DIM]
                rs = lax.rsqrt(jnp.mean(y * y, axis=-1, keepdims=True) + EPS)
                z = y * jnp.where(is_v, 1.0, rs) * gain_ref[0]
                if d == 1:
                    o_ref[0, 0, 0, :, cols] = z.astype(BF16)
                else:
                    y_ref[h] = z
                    for r in range(d):
                        o_ref[0, 0, r, :, cols] = y_ref[h, pl.ds(r, tm // d, stride=d), :].astype(BF16)


def _dil_qkv(x, g, w, gains, *, batch, tm):
    T, D = x.shape
    S = T // batch
    tiles = S // tm
    wcol = lambda i, j: (0, (j % 3) * DIL_GROUPS + j // 3)
    out_shapes, out_specs = [], []
    for group, (_, d) in enumerate(DIL_PAIRS):
        out_shapes.append(jax.ShapeDtypeStruct((3, batch, d, S // d, DIL_GROUP_W), BF16))
        out_specs.append(pl.BlockSpec(
            (1, 1, d, tm // d, DIL_GROUP_W),
            lambda i, j, group=group: (jnp.clip(j - 3 * group, 0, 2), i // tiles, 0, i % tiles, 0)))
    return pl.pallas_call(
        _dil_qkv_kernel,
        out_shape=tuple(out_shapes),
        grid=(T // tm, 3 * DIL_GROUPS),
        in_specs=[
            pl.BlockSpec((tm, D), lambda i, j: (i, 0)),
            pl.BlockSpec((1, D), lambda i, j: (0, 0)),
            pl.BlockSpec((D, DIL_GROUP_W), wcol),
            pl.BlockSpec((1, 1, DIL_HEAD_DIM), lambda i, j: (j % 3, 0, 0)),
        ],
        out_specs=tuple(out_specs),
        scratch_shapes=[pltpu.VMEM((tm, D), BF16), pltpu.VMEM((DIL_HEADS, tm, DIL_HEAD_DIM), F32)],
        compiler_params=_params("arbitrary", "arbitrary"),
        name="dil_qkv",
    )(x, g, w, gains)


def _dil_attn_kernel(q_ref, k_ref, kp_ref, v_ref, vp_ref, bias_ref, o_ref, lse_ref, *, tl):
    i = pl.program_id(2)
    first_pen = jnp.where(i == 0, MASKED, 0.0).astype(F32)
    lane = lax.broadcasted_iota(jnp.int32, (DIL_BLK, LANES), 1)
    q_ref, k_ref, kp_ref, v_ref, vp_ref = (t.at[0, 0, 0] for t in (q_ref, k_ref, kp_ref, v_ref, vp_ref))
    heads = [slice(h * DIL_HEAD_DIM, (h + 1) * DIL_HEAD_DIM) for h in range(DIL_HEADS)]
    for qb in range(tl // DIL_BLK):
        rows = slice(qb * DIL_BLK, (qb + 1) * DIL_BLK)
        both = slice((qb - 1) * DIL_BLK, (qb + 1) * DIL_BLK)
        ss = []
        for h, cols in enumerate(heads):
            q = q_ref[rows, cols]
            if qb == 0:
                s_prev = _dot_nt(q, kp_ref[:, cols]) + first_pen
                s = jnp.concatenate([s_prev, _dot_nt(q, k_ref[rows, cols])], axis=1)
            else:
                s = _dot_nt(q, k_ref[both, cols])
            ss.append(s + bias_ref[h])
        stats = []
        for s in ss:
            m = jnp.max(s, axis=-1, keepdims=True)
            p = jnp.exp(s - m)
            stats.append((m, jnp.sum(p, axis=-1, keepdims=True), p.astype(BF16)))
        lse_tile = jnp.zeros((DIL_BLK, LANES), F32)
        for h, (cols, (m, l, pb)) in enumerate(zip(heads, stats)):
            if qb == 0:
                o = _dot(pb[:, :DIL_BLK], vp_ref[:, cols]) + _dot(pb[:, DIL_BLK:], v_ref[rows, cols])
            else:
                o = _dot(pb, v_ref[both, cols])
            o_ref[0, 0, rows, cols] = (o / l).astype(o_ref.dtype)
            lse_tile = jnp.where(lane == h, m + jnp.log(l), lse_tile)
        lse_ref[0, 0, rows, :] = lse_tile


def _dil_attn(qkv, bias, *, tl):
    _, B, d, L, _ = qkv.shape
    bpt = tl // DIL_BLK
    cur = lambda which: pl.BlockSpec((1, 1, 1, tl, DIL_GROUP_W), lambda b, r, i: (which, b, r, i, 0))
    prev = lambda which: pl.BlockSpec((1, 1, 1, DIL_BLK, DIL_GROUP_W),
                                      lambda b, r, i: (which, b, r, jnp.maximum(i * bpt - 1, 0), 0))
    return pl.pallas_call(
        functools.partial(_dil_attn_kernel, tl=tl),
        out_shape=(
            jax.ShapeDtypeStruct((B, d, L, DIL_GROUP_W), BF16),
            jax.ShapeDtypeStruct((B, d, L, LANES), F32),
        ),
        grid=(B, d, L // tl),
        in_specs=[cur(0), cur(1), prev(1), cur(2), prev(2),
                  pl.BlockSpec(bias.shape, lambda b, r, i: (0, 0, 0))],
        out_specs=(
            pl.BlockSpec((1, 1, tl, DIL_GROUP_W), lambda b, r, i: (b, r, i, 0)),
            pl.BlockSpec((1, 1, tl, LANES), lambda b, r, i: (b, r, i, 0)),
        ),
        compiler_params=_params("parallel", "parallel", "arbitrary"),
        name=f"dil_attn_d{d}",
    )(qkv, qkv, qkv, qkv, qkv, bias)


def _dil_out_kernel(x_ref, o0_ref, o1_ref, o2_ref, l0_ref, l1_ref, l2_ref, w_ref, out_ref,
                    u_ref, lt_ref, mrg_ref):
    tm = x_ref.shape[0]
    for gi, (o_ref, l_ref, (_, d)) in enumerate(zip((o1_ref, o2_ref), (l1_ref, l2_ref), DIL_PAIRS[1:])):
        for r in range(d):
            rows = pl.ds(r, tm // d, stride=d)
            lt_ref[gi, rows, :] = l_ref[0, r]
            for h in range(DIL_HEADS):
                u_ref[gi, h, rows, :] = o_ref[0, r, :, h * DIL_HEAD_DIM:(h + 1) * DIL_HEAD_DIM].astype(F32)
    lses = (l0_ref[0, 0], lt_ref[0], lt_ref[1])
    mx = jnp.maximum(jnp.maximum(lses[0], lses[1]), lses[2])
    es = [jnp.exp(t - mx) for t in lses]
    den = es[0] + es[1] + es[2]
    ws = [e / den for e in es]
    for h in range(DIL_HEADS):
        cols = slice(h * DIL_HEAD_DIM, (h + 1) * DIL_HEAD_DIM)
        acc = ws[0][:, h:h + 1] * o0_ref[0, 0, :, cols].astype(F32)
        for gi in range(DIL_GROUPS - 1):
            acc = acc + ws[gi + 1][:, h:h + 1] * u_ref[gi, h]
        mrg_ref[:, cols] = acc.astype(BF16)
    out_ref[...] = x_ref[...] + _dot(mrg_ref[...], w_ref[...])


def _dil_out(x, os, lses, w, *, batch, tm):
    T, D = x.shape
    tiles = T // batch // tm
    row = lambda i: (i, 0)
    split = lambda d, width: pl.BlockSpec((1, d, tm // d, width), lambda i: (i // tiles, 0, i % tiles, 0))
    return pl.pallas_call(
        _dil_out_kernel,
        out_shape=jax.ShapeDtypeStruct((T, D), F32),
        grid=(T // tm,),
        in_specs=[pl.BlockSpec((tm, D), row)]
        + [split(d, DIL_GROUP_W) for _, d in DIL_PAIRS]
        + [split(d, LANES) for _, d in DIL_PAIRS]
        + [pl.BlockSpec(w.shape, lambda i: (0, 0))],
        out_specs=pl.BlockSpec((tm, D), row),
        scratch_shapes=[
            pltpu.VMEM((DIL_GROUPS - 1, DIL_HEADS, tm, DIL_HEAD_DIM), F32),
            pltpu.VMEM((DIL_GROUPS - 1, tm, LANES), F32),
            pltpu.VMEM((tm, DIL_GROUP_W), BF16),
        ],
        compiler_params=_params("parallel"),
        name="dil_out",
    )(x, *os, *lses, w)


def _pad_rope(t):
    z = jnp.zeros(t.shape[:-1] + (LANES // 2 - ROPE_HALF,), t.dtype)
    return jnp.concatenate([t[..., :ROPE_HALF], z, t[..., ROPE_HALF:], z], axis=-1)


def _pad_qk(t):
    return jnp.concatenate([t[..., :NOPE_DIM], _pad_rope(t[..., NOPE_DIM:])], axis=-1)


def _rope_tables(S):
    inv = 1.0 / (ROPE_THETA ** (jnp.arange(0, ROPE_DIM, 2, dtype=F32) / ROPE_DIM))
    ang = jnp.arange(S, dtype=F32)[:, None] * inv[None, :]
    cos, sin = jnp.cos(ang), jnp.sin(ang)
    return _pad_rope(jnp.concatenate([cos, cos], -1)), _pad_rope(jnp.concatenate([-sin, sin], -1))


def _dil_bias(group, dilation):
    total = DIL_GROUPS * DIL_HEADS
    slopes = 2.0 ** (-8.0 * np.arange(1, total + 1, dtype=np.float32) / total)
    slopes = slopes.reshape(DIL_GROUPS, DIL_HEADS)[group]
    window = DIL_PAIRS[group][0] // dilation
    dist = np.arange(DIL_BLK)[:, None] + DIL_BLK - np.arange(2 * DIL_BLK)[None, :]
    ok = (dist >= 0) & (dist <= window)
    bias = -slopes[:, None, None] * (dilation * dist).astype(np.float32)[None]
    return jnp.asarray(np.where(ok[None], bias, np.float32(MASKED)), dtype=F32)


def _tile(n, want):
    t = min(n, want)
    assert n % t == 0, (n, want)
    return t


def kernel(x, ffn1_norm, ffn1_w_in, ffn1_w_out, mix_norm, ffn2_norm, ffn2_w_in, ffn2_w_out,
           mla_w_down, mla_g_cq, mla_g_ckv, mla_w_uq, mla_w_ukv, mla_g_qn, mla_g_kn, mla_w_o,
           dil_w_qkv, dil_g_qn, dil_g_kn, dil_w_o):
    B, S, D = x.shape
    T = B * S
    depth = ffn1_norm.shape[0]
    F = ffn1_w_out.shape[1]
    tm = _tile(T, 512)
    tf = _tile(F, 512)
    row = lambda v: v.reshape(1, -1).astype(F32)

    ffn_seq = [(norm, w_in, w_out, i) for i in range(depth)
               for norm, w_in, w_out in ((ffn1_norm, ffn1_w_in, ffn1_w_out), (ffn2_norm, ffn2_w_in, ffn2_w_out))]
    ffn_state = {"n": 0, "w": (ffn1_w_in[0].astype(BF16), ffn1_w_out[0].astype(BF16))}

    def ffn(xt):
        n = ffn_state["n"]
        norm, _, _, layer = ffn_seq[n]
        nxt = ffn_seq[n + 1] if n + 1 < len(ffn_seq) else None
        out, cast = _ffn(xt, row(norm[layer]), *ffn_state["w"],
                         next_weights=None if nxt is None else (nxt[1], nxt[2], nxt[3]),
                         tm=_tile(T, 1024), tf=tf)
        ffn_state.update(n=n + 1, w=cast)
        return out

    xt = x.reshape(T, D)
    for i in range(depth):
        j = i // 2
        xt = ffn(xt)
        if i % 2 == 0:
            q_lora, kv_lora = mla_g_cq.shape[1], mla_g_ckv.shape[1]
            heads = mla_w_uq.shape[2] // QK_DIM
            wd = mla_w_down[j]
            wd = jnp.concatenate([wd[:, :q_lora + kv_lora], _pad_rope(wd[:, q_lora + kv_lora:])], -1)
            wuq = _pad_qk(mla_w_uq[j].reshape(q_lora, heads, QK_DIM)).reshape(q_lora, heads * QK_PAD)
            cos, sin = _rope_tables(S)
            q, k, vt = _mla_proj(
                xt.reshape(B, S, D), row(mix_norm[i]), wd.astype(BF16), row(mla_g_cq[j]), row(mla_g_ckv[j]),
                wuq.astype(BF16), mla_w_ukv[j].astype(BF16),
                row(_pad_qk(mla_g_qn[j]) * (LOG2E / math.sqrt(QK_DIM))), row(_pad_qk(mla_g_kn[j])),
                cos, sin, heads=heads, tm=_tile(S, 512), rows=256, tk=_tile(S, 1024))
            o = _mla_attn(q, k, vt, tq=_tile(S, 1024), sub=256)
            xt = _out_proj(xt, o.reshape(T, heads * V_DIM), mla_w_o[j].astype(BF16), tm=tm)
        else:
            gains = jnp.stack([dil_g_qn[j] * (1.0 / math.sqrt(DIL_HEAD_DIM)), dil_g_kn[j],
                               jnp.ones_like(dil_g_kn[j])]).reshape(3, 1, DIL_HEAD_DIM)
            qkvs = _dil_qkv(xt, row(mix_norm[i]), dil_w_qkv[j].astype(BF16), gains, batch=B, tm=_tile(S, 1024))
            os, lses = [], []
            for g, (_, dilation) in enumerate(DIL_PAIRS):
                o_g, lse_g = _dil_attn(qkvs[g], _dil_bias(g, dilation), tl=_tile(S // dilation, 512))
                os.append(o_g)
                lses.append(lse_g)
            xt = _dil_out(xt, os, lses, dil_w_o[j].astype(BF16), batch=B, tm=_tile(S, 512))
        xt = ffn(xt)
    return xt.reshape(B, S, D)
```

```python
import functools
import math

import numpy as np
import jax
import jax.numpy as jnp
from jax import lax
from jax.experimental import pallas as pl
from jax.experimental.pallas import tpu as pltpu

EPS = 1e-6
ROPE_THETA = 10000.0
MASKED = -1e30
LOG2E = math.log2(math.e)

NOPE_DIM = 128
ROPE_DIM = 64
ROPE_HALF = ROPE_DIM // 2
V_DIM = 128
QK_DIM = NOPE_DIM + ROPE_DIM
QK_PAD = 256
DIL_PAIRS = ((128, 1), (512, 4), (2048, 16))
DIL_GROUPS = len(DIL_PAIRS)
DIL_HEADS = 8
DIL_HEAD_DIM = 128
DIL_BLK = 128
DIL_GROUP_W = DIL_HEADS * DIL_HEAD_DIM

LANES = 128
SUBLANES = 8
VMEM_LIMIT = 56 * 1024 * 1024

BF16 = jnp.bfloat16
F32 = jnp.float32


def _params(*semantics, flags=None):
    return pltpu.CompilerParams(dimension_semantics=semantics, vmem_limit_bytes=VMEM_LIMIT, flags=flags)


def _rms(t):
    return t * lax.rsqrt(jnp.mean(t * t, axis=-1, keepdims=True) + EPS)


def _dot(a, b):
    return jnp.dot(a, b, preferred_element_type=F32)


def _dot_nt(a, b):
    return lax.dot_general(a, b, (((1,), (1,)), ((), ())), preferred_element_type=F32)


def _ffn_kernel(x_hbm, g_ref, wg_ref, wu_ref, wo_ref, *rest, cast_next):
    if cast_next:
        nwi_ref, nwo_ref, o_ref, cwi_ref, cwo_ref, x_buf, xn_ref, sem = rest
        cwi_ref[...] = nwi_ref[0].astype(BF16)
        cwo_ref[...] = nwo_ref[0].astype(BF16)
    else:
        o_ref, x_buf, xn_ref, sem = rest
    i, j = pl.program_id(0), pl.program_id(1)
    tm = o_ref.shape[0]
    prefetch_step = min(1, pl.num_programs(1) - 1)

    def x_copy(tile):
        return pltpu.make_async_copy(x_hbm.at[pl.ds(tile * tm, tm), :], x_buf, sem)

    @pl.when(jnp.logical_and(i == 0, j == 0))
    def _():
        x_copy(0).start()

    @pl.when(j == 0)
    def _():
        x_copy(i).wait()
        x = x_buf[...]
        xn_ref[...] = (_rms(x) * g_ref[...]).astype(BF16)
        o_ref[...] = x

    @pl.when(jnp.logical_and(j == prefetch_step, i + 1 < pl.num_programs(0)))
    def _():
        x_copy(i + 1).start()

    xn = xn_ref[...]
    gate = _dot(xn, wg_ref[...])
    up = _dot(xn, wu_ref[...])
    h = (gate * jax.nn.sigmoid(gate) * up).astype(BF16)
    o_ref[...] += 0.5 * _dot(h, wo_ref[...])


def _ffn(x, g, w_in, w_out, next_weights=None, *, tm, tf):
    T, D = x.shape
    F = w_out.shape[0]
    nt, nf = T // tm, F // tf
    in_specs = [
        pl.BlockSpec(memory_space=pl.ANY),
        pl.BlockSpec((1, D), lambda i, j: (0, 0)),
        pl.BlockSpec((D, tf), lambda i, j: (0, j)),
        pl.BlockSpec((D, tf), lambda i, j: (0, j + nf)),
        pl.BlockSpec((tf, D), lambda i, j: (j, 0)),
    ]
    out_shape = [jax.ShapeDtypeStruct((T, D), F32)]
    out_specs = [pl.BlockSpec((tm, D), lambda i, j: (i, 0))]
    operands = [x, g, w_in, w_in, w_out]
    if next_weights is not None:
        nwi, nwo, layer = next_weights
        in_specs += [pl.BlockSpec((1, D // nt, 2 * F // nf), lambda i, j: (layer, i, j)),
                     pl.BlockSpec((1, F // nf, D // nt), lambda i, j: (layer, j, i))]
        out_shape += [jax.ShapeDtypeStruct(nwi.shape[1:], BF16), jax.ShapeDtypeStruct(nwo.shape[1:], BF16)]
        out_specs += [pl.BlockSpec((D // nt, 2 * F // nf), lambda i, j: (i, j)),
                      pl.BlockSpec((F // nf, D // nt), lambda i, j: (j, i))]
        operands += [nwi, nwo]
    outs = pl.pallas_call(
        functools.partial(_ffn_kernel, cast_next=next_weights is not None),
        out_shape=tuple(out_shape),
        grid=(nt, nf),
        in_specs=in_specs,
        out_specs=tuple(out_specs),
        scratch_shapes=[pltpu.VMEM((tm, D), F32), pltpu.VMEM((tm, D), BF16), pltpu.SemaphoreType.DMA],
        compiler_params=_params("arbitrary", "arbitrary"),
        name="ffn",
    )(*operands)
    return outs[0], tuple(outs[1:])


def _rope(t, cos, sin):
    return t * cos + pltpu.roll(t, LANES // 2, 1) * sin


def _mla_proj_kernel(x_ref, g_ref, wd_ref, gcq_ref, gckv_ref, wuq_ref, wukv_ref, gq_ref, gk_ref,
                     cos_ref, sin_ref, q_ref, k_ref, v_ref, *, heads, q_lora, kv_lora, rows):
    chains = [slice(c * rows, (c + 1) * rows) for c in range(x_ref.shape[1] // rows)]
    gq_n, gq_r = gq_ref[:, :NOPE_DIM], gq_ref[:, NOPE_DIM:]
    gk_n, gk_r = gk_ref[:, :NOPE_DIM], gk_ref[:, NOPE_DIM:]
    xns = [(_rms(x_ref[0, rs, :]) * g_ref[...]).astype(BF16) for rs in chains]
    lats = [_dot(xn, wd_ref[...]) for xn in xns]
    cqs = [(_rms(lat[:, :q_lora]) * gcq_ref[...]).astype(BF16) for lat in lats]
    ckvs = [(_rms(lat[:, q_lora:q_lora + kv_lora]) * gckv_ref[...]).astype(BF16) for lat in lats]
    qs = [[_dot(cq, wuq_ref[:, h * QK_PAD:(h + 1) * QK_PAD]) for h in range(heads)] for cq in cqs]
    kvs = [[_dot(ckv, wukv_ref[:, h * QK_PAD:(h + 1) * QK_PAD]) for h in range(heads)] for ckv in ckvs]
    for c, rs in enumerate(chains):
        cos, sin = cos_ref[rs, :], sin_ref[rs, :]
        kpe = lats[c][:, q_lora + kv_lora:]
        kpe_ssq = jnp.sum(kpe * kpe, axis=-1, keepdims=True)
        kpe_rot = _rope(kpe * gk_r, cos, sin)
        for h in range(heads):
            q = qs[c][h]
            qn, qr = q[:, :NOPE_DIM], q[:, NOPE_DIM:]
            ssq = jnp.sum(qn * qn + qr * qr, axis=-1, keepdims=True)
            rsq = lax.rsqrt(ssq * (1.0 / QK_DIM) + EPS)
            q_ref[0, h, rs, :NOPE_DIM] = (qn * rsq * gq_n).astype(BF16)
            q_ref[0, h, rs, NOPE_DIM:] = _rope(qr * rsq * gq_r, cos, sin).astype(BF16)
            kv = kvs[c][h]
            kn = kv[:, :NOPE_DIM]
            ssk = jnp.sum(kn * kn, axis=-1, keepdims=True) + kpe_ssq
            rsk = lax.rsqrt(ssk * (1.0 / QK_DIM) + EPS)
            k_ref[0, h, rs, :NOPE_DIM] = (kn * rsk * gk_n).astype(BF16)
            k_ref[0, h, rs, NOPE_DIM:] = (kpe_rot * rsk).astype(BF16)
            v_ref[0, h, 0, :, rs] = kv[:, NOPE_DIM:].T.astype(BF16)


def _mla_proj(x, g, wd, gcq, gckv, wuq, wukv, gq, gk, cos, sin, *, heads, tm, rows, tk):
    B, S, D = x.shape
    q_lora, kv_lora = gcq.shape[1], gckv.shape[1]
    const = lambda b, i: (0, 0)
    once = pl.Buffered(1)
    kern = functools.partial(_mla_proj_kernel, heads=heads, q_lora=q_lora, kv_lora=kv_lora, rows=rows)
    return pl.pallas_call(
        kern,
        out_shape=(
            jax.ShapeDtypeStruct((B, heads, S, QK_PAD), BF16),
            jax.ShapeDtypeStruct((B, heads, S, QK_PAD), BF16),
            jax.ShapeDtypeStruct((B, heads, S // tk, V_DIM, tk), BF16),
        ),
        grid=(B, S // tm),
        in_specs=[
            pl.BlockSpec((1, tm, D), lambda b, i: (b, i, 0)),
            pl.BlockSpec(g.shape, const),
            pl.BlockSpec(wd.shape, const, pipeline_mode=once),
            pl.BlockSpec(gcq.shape, const),
            pl.BlockSpec(gckv.shape, const),
            pl.BlockSpec(wuq.shape, const, pipeline_mode=once),
            pl.BlockSpec(wukv.shape, const, pipeline_mode=once),
            pl.BlockSpec(gq.shape, const),
            pl.BlockSpec(gk.shape, const),
            pl.BlockSpec((tm, LANES), lambda b, i: (i, 0)),
            pl.BlockSpec((tm, LANES), lambda b, i: (i, 0)),
        ],
        out_specs=(
            pl.BlockSpec((1, heads, tm, QK_PAD), lambda b, i: (b, 0, i, 0)),
            pl.BlockSpec((1, heads, tm, QK_PAD), lambda b, i: (b, 0, i, 0)),
            pl.BlockSpec((1, heads, 1, V_DIM, tm), lambda b, i: (b, 0, i // (tk // tm), 0, i % (tk // tm))),
        ),
        compiler_params=_params("parallel", "parallel"),
        name="mla_proj",
    )(x, g, wd, gcq, gckv, wuq, wukv, gq, gk, cos, sin)


def _mla_attn_kernel(q_ref, k_ref, vt_ref, o_ref, m_ref, l_ref, acc_ref, s_ref, *, tq, tk, sub):
    qi = pl.program_id(2)
    n = tq // sub
    diag_chunks = tq // tk
    assert diag_chunks % 2 == 0
    m_ref[...] = jnp.full(m_ref.shape, MASKED, F32)
    l_ref[...] = jnp.zeros(l_ref.shape, F32)
    acc_ref[...] = jnp.zeros(acc_ref.shape, F32)

    def scores(c, slot, first_chain=0):
        k = k_ref[0, 0, pl.ds(pl.multiple_of(c * tk, tk), tk), :]
        for sb in range(first_chain, n):
            qs = slice(sb * sub, (sb + 1) * sub)
            s_ref[slot, :, qs] = _dot_nt(k, q_ref[0, 0, qs, :])

    def consume(c, slot, diag=None):
        for sb in range(n):
            qs = slice(sb * sub, (sb + 1) * sub)
            keys = tk if diag is None else min(tk, (sb + 1) * sub - diag * tk)
            if keys <= 0:
                continue
            st = s_ref[slot, :keys, qs]
            if diag is not None and diag * tk + keys - 1 > sb * sub:
                key = lax.broadcasted_iota(jnp.int32, st.shape, 0) + diag * tk
                qry = lax.broadcasted_iota(jnp.int32, st.shape, 1) + sb * sub
                st = jnp.where(key <= qry, st, MASKED)
            m_old = m_ref[:, qs]
            m_new = jnp.maximum(m_old, jnp.max(st, axis=0, keepdims=True))
            alpha = jnp.exp2(m_old - m_new)
            p = jnp.exp2(st - m_new)
            l_ref[:, qs] = alpha * l_ref[:, qs] + jnp.sum(p, axis=0, keepdims=True)
            m_ref[:, qs] = m_new
            acc_ref[:, qs] = alpha * acc_ref[:, qs] + _dot(vt_ref[0, 0, c, :, :keys], p.astype(BF16))

    chunks = qi * diag_chunks
    scores(0, 0)

    def body(t, carry):
        scores(2 * t + 1, 1)
        consume(2 * t, 0)
        scores(2 * t + 2, 0)
        consume(2 * t + 1, 1)
        return carry

    lax.fori_loop(0, chunks // 2, body, 0)
    for dc in range(diag_chunks):
        if dc + 1 < diag_chunks:
            scores(chunks + dc + 1, (dc + 1) % 2, first_chain=(dc + 1) * tk // sub)
        consume(chunks + dc, dc % 2, diag=dc)
    o_ref[0] = (acc_ref[...] / l_ref[...]).T.astype(o_ref.dtype)


def _mla_attn(q, k, vt, *, tq, sub):
    B, H, S, _ = q.shape
    nc, _, tk = vt.shape[2:]
    return pl.pallas_call(
        functools.partial(_mla_attn_kernel, tq=tq, tk=tk, sub=sub),
        out_shape=jax.ShapeDtypeStruct((B, S, H * V_DIM), BF16),
        grid=(B, H, S // tq),
        in_specs=[
            pl.BlockSpec((1, 1, tq, QK_PAD), lambda b, h, i: (b, h, i, 0)),
            pl.BlockSpec((1, 1, S, QK_PAD), lambda b, h, i: (b, h, 0, 0)),
            pl.BlockSpec((1, 1, nc, V_DIM, tk), lambda b, h, i: (b, h, 0, 0, 0)),
        ],
        out_specs=pl.BlockSpec((1, tq, V_DIM), lambda b, h, i: (b, i, h)),
        scratch_shapes=[
            pltpu.VMEM((1, tq), F32),
            pltpu.VMEM((1, tq), F32),
            pltpu.VMEM((V_DIM, tq), F32),
            pltpu.VMEM((2, tk, tq), F32),
        ],
        compiler_params=_params("parallel", "parallel", "arbitrary"),
        name="mla_attn",
    )(q, k, vt)


def _out_proj_kernel(x_ref, o_ref, w_ref, out_ref):
    out_ref[...] = x_ref[...] + _dot(o_ref[...], w_ref[...])


def _out_proj(x, o, w, *, tm):
    T, D = x.shape
    return pl.pallas_call(
        _out_proj_kernel,
        out_shape=jax.ShapeDtypeStruct((T, D), F32),
        grid=(T // tm,),
        in_specs=[
            pl.BlockSpec((tm, D), lambda i: (i, 0)),
            pl.BlockSpec((tm, o.shape[1]), lambda i: (i, 0)),
            pl.BlockSpec(w.shape, lambda i: (0, 0)),
        ],
        out_specs=pl.BlockSpec((tm, D), lambda i: (i, 0)),
        compiler_params=_params("parallel"),
        name="out_proj",
    )(x, o, w)


def _dil_qkv_kernel(x_ref, g_ref, w_ref, gain_ref, o0_ref, o1_ref, o2_ref, xn_ref, y_ref):
    j = pl.program_id(1)
    tm = x_ref.shape[0]

    @pl.when(j == 0)
    def _():
        xn_ref[...] = (_rms(x_ref[...]) * g_ref[...]).astype(BF16)

    is_v = j % 3 == 2
    pair_w = 2 * DIL_HEAD_DIM

    for group, (o_ref, (_, d)) in enumerate(zip((o0_ref, o1_ref, o2_ref), DIL_PAIRS)):
        @pl.when(j // 3 == group)
        def _(o_ref=o_ref, d=d):
            xn = xn_ref[...]
            ys = [_dot(xn, w_ref[:, c * pair_w:(c + 1) * pair_w]) for c in range(DIL_HEADS // 2)]
            for h in range(DIL_HEADS):
                cols = slice(h * DIL_HEAD_DIM, (h + 1) * DIL_HEAD_DIM)
                y = ys[h // 2][:, (h % 2) * DIL_HEAD_DIM:(h % 2 + 1) * DIL_HEAD_DIM]
                rs = lax.rsqrt(jnp.mean(y * y, axis=-1, keepdims=True) + EPS)
                z = y * jnp.where(is_v, 1.0, rs) * gain_ref[0]
                if d == 1:
                    o_ref[0, 0, 0, :, cols] = z.astype(BF16)
                elif d % (2 * SUBLANES) == 0:
                    halves = z.reshape(tm // (2 * SUBLANES), 2, SUBLANES, DIL_HEAD_DIM)
                    for half in range(2):
                        y_ref[h, half * (tm // 2):(half + 1) * (tm // 2)] = (
                            halves[:, half].reshape(tm // 2, DIL_HEAD_DIM))
                    for r in range(d):
                        chunk, row = divmod(r % (2 * SUBLANES), SUBLANES)
                        start = chunk * (tm // 2) + (r // (2 * SUBLANES)) * SUBLANES + row
                        o_ref[0, 0, r, :, cols] = y_ref[h, pl.ds(start, tm // d, stride=d // 2), :].astype(BF16)
                else:
                    y_ref[h] = z
                    for r in range(d):
                        o_ref[0, 0, r, :, cols] = y_ref[h, pl.ds(r, tm // d, stride=d), :].astype(BF16)


def _dil_qkv(x, g, w, gains, *, batch, tm):
    T, D = x.shape
    S = T // batch
    tiles = S // tm
    wcol = lambda i, j: (0, (j % 3) * DIL_GROUPS + j // 3)
    out_shapes, out_specs = [], []
    for group, (_, d) in enumerate(DIL_PAIRS):
        out_shapes.append(jax.ShapeDtypeStruct((3, batch, d, S // d, DIL_GROUP_W), BF16))
        out_specs.append(pl.BlockSpec(
            (1, 1, d, tm // d, DIL_GROUP_W),
            lambda i, j, group=group: (jnp.clip(j - 3 * group, 0, 2), i // tiles, 0, i % tiles, 0)))
    return pl.pallas_call(
        _dil_qkv_kernel,
        out_shape=tuple(out_shapes),
        grid=(T // tm, 3 * DIL_GROUPS),
        in_specs=[
            pl.BlockSpec((tm, D), lambda i, j: (i, 0)),
            pl.BlockSpec((1, D), lambda i, j: (0, 0)),
            pl.BlockSpec((D, DIL_GROUP_W), wcol),
            pl.BlockSpec((1, 1, DIL_HEAD_DIM), lambda i, j: (j % 3, 0, 0)),
        ],
        out_specs=tuple(out_specs),
        scratch_shapes=[pltpu.VMEM((tm, D), BF16), pltpu.VMEM((DIL_HEADS, tm, DIL_HEAD_DIM), F32)],
        compiler_params=_params("arbitrary", "arbitrary"),
        name="dil_qkv",
    )(x, g, w, gains)


def _dil_attn_kernel(q_ref, k_ref, kp_ref, v_ref, vp_ref, bias_ref, o_ref, lse_ref, *, tl):
    i = pl.program_id(2)
    first_pen = jnp.where(i == 0, MASKED, 0.0).astype(F32)
    lane = lax.broadcasted_iota(jnp.int32, (DIL_BLK, LANES), 1)
    q_ref, k_ref, kp_ref, v_ref, vp_ref = (t.at[0, 0, 0] for t in (q_ref, k_ref, kp_ref, v_ref, vp_ref))
    heads = [slice(h * DIL_HEAD_DIM, (h + 1) * DIL_HEAD_DIM) for h in range(DIL_HEADS)]
    for qb in range(tl // DIL_BLK):
        rows = slice(qb * DIL_BLK, (qb + 1) * DIL_BLK)
        both = slice((qb - 1) * DIL_BLK, (qb + 1) * DIL_BLK)
        ss = []
        for h, cols in enumerate(heads):
            q = q_ref[rows, cols]
            if qb == 0:
                s_prev = _dot_nt(q, kp_ref[:, cols]) + first_pen
                s = jnp.concatenate([s_prev, _dot_nt(q, k_ref[rows, cols])], axis=1)
            else:
                s = _dot_nt(q, k_ref[both, cols])
            ss.append(s + bias_ref[h])
        stats = []
        for s in ss:
            m = jnp.max(s, axis=-1, keepdims=True)
            p = jnp.exp(s - m)
            stats.append((m, jnp.sum(p, axis=-1, keepdims=True), p.astype(BF16)))
        lse_tile = jnp.zeros((DIL_BLK, LANES), F32)
        for h, (cols, (m, l, pb)) in enumerate(zip(heads, stats)):
            if qb == 0:
                o = _dot(pb[:, :DIL_BLK], vp_ref[:, cols]) + _dot(pb[:, DIL_BLK:], v_ref[rows, cols])
            else:
                o = _dot(pb, v_ref[both, cols])
            o_ref[0, 0, rows, cols] = (o / l).astype(o_ref.dtype)
            lse_tile = jnp.where(lane == h, m + jnp.log(l), lse_tile)
        lse_ref[0, 0, rows, :] = lse_tile


def _dil_attn(qkv, bias, *, tl):
    _, B, d, L, _ = qkv.shape
    bpt = tl // DIL_BLK
    cur = lambda which: pl.BlockSpec((1, 1, 1, tl, DIL_GROUP_W), lambda b, r, i: (which, b, r, i, 0))
    prev = lambda which: pl.BlockSpec((1, 1, 1, DIL_BLK, DIL_GROUP_W),
                                      lambda b, r, i: (which, b, r, jnp.maximum(i * bpt - 1, 0), 0))
    return pl.pallas_call(
        functools.partial(_dil_attn_kernel, tl=tl),
        out_shape=(
            jax.ShapeDtypeStruct((B, d, L, DIL_GROUP_W), BF16),
            jax.ShapeDtypeStruct((B, d, L, LANES), F32),
        ),
        grid=(B, d, L // tl),
        in_specs=[cur(0), cur(1), prev(1), cur(2), prev(2),
                  pl.BlockSpec(bias.shape, lambda b, r, i: (0, 0, 0))],
        out_specs=(
            pl.BlockSpec((1, 1, tl, DIL_GROUP_W), lambda b, r, i: (b, r, i, 0)),
            pl.BlockSpec((1, 1, tl, LANES), lambda b, r, i: (b, r, i, 0)),
        ),
        compiler_params=_params("parallel", "parallel", "arbitrary"),
        name=f"dil_attn_d{d}",
    )(qkv, qkv, qkv, qkv, qkv, bias)


def _dil_out_kernel(x_ref, o0_ref, o1_ref, o2_ref, l0_ref, l1_ref, l2_ref, w_ref, out_ref,
                    u_ref, lt_ref):
    tm = x_ref.shape[0]
    for gi, (o_ref, l_ref, (_, d)) in enumerate(zip((o1_ref, o2_ref), (l1_ref, l2_ref), DIL_PAIRS[1:])):
        for r in range(d):
            rows = pl.ds(r, tm // d, stride=d)
            lt_ref[gi, rows, :] = l_ref[0, r]
            for h in range(DIL_HEADS):
                u_ref[gi, h, rows, :] = o_ref[0, r, :, h * DIL_HEAD_DIM:(h + 1) * DIL_HEAD_DIM].astype(F32)
    lses = (l0_ref[0, 0], lt_ref[0], lt_ref[1])
    mx = jnp.maximum(jnp.maximum(lses[0], lses[1]), lses[2])
    es = [jnp.exp(t - mx) for t in lses]
    den = es[0] + es[1] + es[2]
    ws = [e / den for e in es]
    out = x_ref[...]
    for pair in range(DIL_HEADS // 2):
        merged = []
        for h in (2 * pair, 2 * pair + 1):
            acc = ws[0][:, h:h + 1] * o0_ref[0, 0, :, h * DIL_HEAD_DIM:(h + 1) * DIL_HEAD_DIM].astype(F32)
            for gi in range(DIL_GROUPS - 1):
                acc = acc + ws[gi + 1][:, h:h + 1] * u_ref[gi, h]
            merged.append(acc.astype(BF16))
        rows = slice(2 * pair * DIL_HEAD_DIM, 2 * (pair + 1) * DIL_HEAD_DIM)
        out = out + _dot(jnp.concatenate(merged, axis=1), w_ref[rows, :])
    out_ref[...] = out


def _dil_out(x, os, lses, w, *, batch, tm):
    T, D = x.shape
    tiles = T // batch // tm
    row = lambda i: (i, 0)
    split = lambda d, width: pl.BlockSpec((1, d, tm // d, width), lambda i: (i // tiles, 0, i % tiles, 0))
    return pl.pallas_call(
        _dil_out_kernel,
        out_shape=jax.ShapeDtypeStruct((T, D), F32),
        grid=(T // tm,),
        in_specs=[pl.BlockSpec((tm, D), row)]
        + [split(d, DIL_GROUP_W) for _, d in DIL_PAIRS]
        + [split(d, LANES) for _, d in DIL_PAIRS]
        + [pl.BlockSpec(w.shape, lambda i: (0, 0))],
        out_specs=pl.BlockSpec((tm, D), row),
        scratch_shapes=[
            pltpu.VMEM((DIL_GROUPS - 1, DIL_HEADS, tm, DIL_HEAD_DIM), F32),
            pltpu.VMEM((DIL_GROUPS - 1, tm, LANES), F32),
        ],
        compiler_params=_params("parallel"),
        name="dil_out",
    )(x, *os, *lses, w)


def _pad_rope(t):
    z = jnp.zeros(t.shape[:-1] + (LANES // 2 - ROPE_HALF,), t.dtype)
    return jnp.concatenate([t[..., :ROPE_HALF], z, t[..., ROPE_HALF:], z], axis=-1)


def _pad_qk(t):
    return jnp.concatenate([t[..., :NOPE_DIM], _pad_rope(t[..., NOPE_DIM:])], axis=-1)


def _rope_tables(S):
    inv = 1.0 / (ROPE_THETA ** (jnp.arange(0, ROPE_DIM, 2, dtype=F32) / ROPE_DIM))
    ang = jnp.arange(S, dtype=F32)[:, None] * inv[None, :]
    cos, sin = jnp.cos(ang), jnp.sin(ang)
    return _pad_rope(jnp.concatenate([cos, cos], -1)), _pad_rope(jnp.concatenate([-sin, sin], -1))


def _dil_bias(group, dilation):
    total = DIL_GROUPS * DIL_HEADS
    slopes = 2.0 ** (-8.0 * np.arange(1, total + 1, dtype=np.float32) / total)
    slopes = slopes.reshape(DIL_GROUPS, DIL_HEADS)[group]
    window = DIL_PAIRS[group][0] // dilation
    dist = np.arange(DIL_BLK)[:, None] + DIL_BLK - np.arange(2 * DIL_BLK)[None, :]
    ok = (dist >= 0) & (dist <= window)
    bias = -slopes[:, None, None] * (dilation * dist).astype(np.float32)[None]
    return jnp.asarray(np.where(ok[None], bias, np.float32(MASKED)), dtype=F32)


def _tile(n, want):
    t = min(n, want)
    assert n % t == 0, (n, want)
    return t


def kernel(x, ffn1_norm, ffn1_w_in, ffn1_w_out, mix_norm, ffn2_norm, ffn2_w_in, ffn2_w_out,
           mla_w_down, mla_g_cq, mla_g_ckv, mla_w_uq, mla_w_ukv, mla_g_qn, mla_g_kn, mla_w_o,
           dil_w_qkv, dil_g_qn, dil_g_kn, dil_w_o):
    B, S, D = x.shape
    T = B * S
    depth = ffn1_norm.shape[0]
    F = ffn1_w_out.shape[1]
    tm = _tile(T, 512)
    tf = _tile(F, 512)
    row = lambda v: v.reshape(1, -1).astype(F32)

    ffn_seq = [(norm, w_in, w_out, i) for i in range(depth)
               for norm, w_in, w_out in ((ffn1_norm, ffn1_w_in, ffn1_w_out), (ffn2_norm, ffn2_w_in, ffn2_w_out))]
    ffn_state = {"n": 0, "w": (ffn1_w_in[0].astype(BF16), ffn1_w_out[0].astype(BF16))}

    def ffn(xt):
        n = ffn_state["n"]
        norm, _, _, layer = ffn_seq[n]
        nxt = ffn_seq[n + 1] if n + 1 < len(ffn_seq) else None
        out, cast = _ffn(xt, row(norm[layer]), *ffn_state["w"],
                         next_weights=None if nxt is None else (nxt[1], nxt[2], nxt[3]),
                         tm=_tile(T, 1024), tf=tf)
        ffn_state.update(n=n + 1, w=cast)
        return out

    xt = x.reshape(T, D)
    for i in range(depth):
        j = i // 2
        xt = ffn(xt)
        if i % 2 == 0:
            q_lora, kv_lora = mla_g_cq.shape[1], mla_g_ckv.shape[1]
            heads = mla_w_uq.shape[2] // QK_DIM
            wd = mla_w_down[j]
            wd = jnp.concatenate([wd[:, :q_lora + kv_lora], _pad_rope(wd[:, q_lora + kv_lora:])], -1)
            wuq = _pad_qk(mla_w_uq[j].reshape(q_lora, heads, QK_DIM)).reshape(q_lora, heads * QK_PAD)
            cos, sin = _rope_tables(S)
            q, k, vt = _mla_proj(
                xt.reshape(B, S, D), row(mix_norm[i]), wd.astype(BF16), row(mla_g_cq[j]), row(mla_g_ckv[j]),
                wuq.astype(BF16), mla_w_ukv[j].astype(BF16),
                row(_pad_qk(mla_g_qn[j]) * (LOG2E / math.sqrt(QK_DIM))), row(_pad_qk(mla_g_kn[j])),
                cos, sin, heads=heads, tm=_tile(S, 512), rows=256, tk=_tile(S, 512))
            o = _mla_attn(q, k, vt, tq=_tile(S, 1024), sub=256)
            xt = _out_proj(xt, o.reshape(T, heads * V_DIM), mla_w_o[j].astype(BF16), tm=tm)
        else:
            gains = jnp.stack([dil_g_qn[j] * (1.0 / math.sqrt(DIL_HEAD_DIM)), dil_g_kn[j],
                               jnp.ones_like(dil_g_kn[j])]).reshape(3, 1, DIL_HEAD_DIM)
            qkvs = _dil_qkv(xt, row(mix_norm[i]), dil_w_qkv[j].astype(BF16), gains, batch=B, tm=_tile(S, 1024))
            os, lses = [], []
            for g, (_, dilation) in enumerate(DIL_PAIRS):
                o_g, lse_g = _dil_attn(qkvs[g], _dil_bias(g, dilation), tl=_tile(S // dilation, 512))
                os.append(o_g)
                lses.append(lse_g)
            xt = _dil_out(xt, os, lses, dil_w_o[j].astype(BF16), batch=B, tm=_tile(S, 512))
        xt = ffn(xt)
    return xt.reshape(B, S, D)
```

```python
import functools
import math

import numpy as np
import jax
import jax.numpy as jnp
from jax import lax
from jax.experimental import pallas as pl
from jax.experimental.pallas import tpu as pltpu

EPS = 1e-6
ROPE_THETA = 10000.0
MASKED = -1e30
LOG2E = math.log2(math.e)

NOPE_DIM = 128
ROPE_DIM = 64
ROPE_HALF = ROPE_DIM // 2
V_DIM = 128
QK_DIM = NOPE_DIM + ROPE_DIM
QK_PAD = 256
DIL_PAIRS = ((128, 1), (512, 4), (2048, 16))
DIL_GROUPS = len(DIL_PAIRS)
DIL_HEADS = 8
DIL_HEAD_DIM = 128
DIL_BLK = 128
DIL_GROUP_W = DIL_HEADS * DIL_HEAD_DIM

LANES = 128
SUBLANES = 8
VMEM_LIMIT = 56 * 1024 * 1024

BF16 = jnp.bfloat16
F32 = jnp.float32


def _params(*semantics, flags=None):
    return pltpu.CompilerParams(dimension_semantics=semantics, vmem_limit_bytes=VMEM_LIMIT, flags=flags)


def _rms(t):
    return t * lax.rsqrt(jnp.mean(t * t, axis=-1, keepdims=True) + EPS)


def _dot(a, b):
    return jnp.dot(a, b, preferred_element_type=F32)


def _dot_nt(a, b):
    return lax.dot_general(a, b, (((1,), (1,)), ((), ())), preferred_element_type=F32)


def _ffn_kernel(x_hbm, g_ref, wg_ref, wu_ref, wo_ref, *rest, cast_next):
    src_refs, o_ref, dst_refs = rest[:cast_next], rest[cast_next], rest[cast_next + 1:2 * cast_next + 1]
    x_buf, xn_ref, sem = rest[2 * cast_next + 1:]
    for src, dst in zip(src_refs, dst_refs):
        dst[...] = src[0].astype(BF16)
    i, j = pl.program_id(0), pl.program_id(1)
    tm = o_ref.shape[0]
    prefetch_step = min(1, pl.num_programs(1) - 1)

    def x_copy(tile):
        return pltpu.make_async_copy(x_hbm.at[pl.ds(tile * tm, tm), :], x_buf, sem)

    @pl.when(jnp.logical_and(i == 0, j == 0))
    def _():
        x_copy(0).start()

    @pl.when(j == 0)
    def _():
        x_copy(i).wait()
        x = x_buf[...]
        xn_ref[...] = (_rms(x) * g_ref[...]).astype(BF16)
        o_ref[...] = x

    @pl.when(jnp.logical_and(j == prefetch_step, i + 1 < pl.num_programs(0)))
    def _():
        x_copy(i + 1).start()

    xn = xn_ref[...]
    gate = _dot(xn, wg_ref[...])
    up = _dot(xn, wu_ref[...])
    h = (gate * jax.nn.sigmoid(gate) * up).astype(BF16)
    o_ref[...] += 0.5 * _dot(h, wo_ref[...])


def _cast_blocks(shape, nt, nf, rows_by_tile):
    R, C = shape
    if rows_by_tile:
        k = max(k for k in range(1, nf + 1) if C % k == 0 and (C // k) % LANES == 0)
        return (R // nt, C // k), lambda i, j: (i, jnp.minimum(j, k - 1))
    k = max(k for k in range(1, nf + 1) if R % k == 0 and (R // k) % (2 * SUBLANES) == 0)
    return (R // k, C // nt), lambda i, j: (jnp.minimum(j, k - 1), i)


def _ffn(x, g, w_in, w_out, casts=(), *, tm, tf):
    T, D = x.shape
    F = w_out.shape[0]
    nt, nf = T // tm, F // tf
    in_specs = [
        pl.BlockSpec(memory_space=pl.ANY),
        pl.BlockSpec((1, D), lambda i, j: (0, 0)),
        pl.BlockSpec((D, tf), lambda i, j: (0, j)),
        pl.BlockSpec((D, tf), lambda i, j: (0, j + nf)),
        pl.BlockSpec((tf, D), lambda i, j: (j, 0)),
    ]
    out_shape = [jax.ShapeDtypeStruct((T, D), F32)]
    out_specs = [pl.BlockSpec((tm, D), lambda i, j: (i, 0))]
    operands = [x, g, w_in, w_in, w_out]
    for stack, layer, rows_by_tile in casts:
        block, index = _cast_blocks(stack.shape[1:], nt, nf, rows_by_tile)
        in_specs.append(pl.BlockSpec((1,) + block, lambda i, j, layer=layer, index=index: (layer,) + index(i, j)))
        out_shape.append(jax.ShapeDtypeStruct(stack.shape[1:], BF16))
        out_specs.append(pl.BlockSpec(block, index))
        operands.append(stack)
    outs = pl.pallas_call(
        functools.partial(_ffn_kernel, cast_next=len(casts)),
        out_shape=tuple(out_shape),
        grid=(nt, nf),
        in_specs=in_specs,
        out_specs=tuple(out_specs),
        scratch_shapes=[pltpu.VMEM((tm, D), F32), pltpu.VMEM((tm, D), BF16), pltpu.SemaphoreType.DMA],
        compiler_params=_params("arbitrary", "arbitrary"),
        name="ffn",
    )(*operands)
    return outs[0], tuple(outs[1:])


def _rope(t, cos, sin):
    return t * cos + pltpu.roll(t, LANES // 2, 1) * sin


def _mla_proj_kernel(x_ref, g_ref, wd_ref, gcq_ref, gckv_ref, wuq_ref, wukv_ref, gq_ref, gk_ref,
                     cos_ref, sin_ref, q_ref, k_ref, v_ref, *, heads, q_lora, kv_lora, rows):
    chains = [slice(c * rows, (c + 1) * rows) for c in range(x_ref.shape[1] // rows)]
    gq_n, gq_r = gq_ref[:, :NOPE_DIM], gq_ref[:, NOPE_DIM:]
    gk_n, gk_r = gk_ref[:, :NOPE_DIM], gk_ref[:, NOPE_DIM:]
    xns = [(_rms(x_ref[0, rs, :]) * g_ref[...]).astype(BF16) for rs in chains]
    lats = [_dot(xn, wd_ref[...]) for xn in xns]
    cqs = [(_rms(lat[:, :q_lora]) * gcq_ref[...]).astype(BF16) for lat in lats]
    ckvs = [(_rms(lat[:, q_lora:q_lora + kv_lora]) * gckv_ref[...]).astype(BF16) for lat in lats]
    qs = [[_dot(cq, wuq_ref[:, h * QK_PAD:(h + 1) * QK_PAD]) for h in range(heads)] for cq in cqs]
    kvs = [[_dot(ckv, wukv_ref[:, h * QK_PAD:(h + 1) * QK_PAD]) for h in range(heads)] for ckv in ckvs]
    for c, rs in enumerate(chains):
        cos, sin = cos_ref[rs, :], sin_ref[rs, :]
        kpe = lats[c][:, q_lora + kv_lora:]
        kpe_ssq = jnp.sum(kpe * kpe, axis=-1, keepdims=True)
        kpe_rot = _rope(kpe * gk_r, cos, sin)
        for h in range(heads):
            q = qs[c][h]
            qn, qr = q[:, :NOPE_DIM], q[:, NOPE_DIM:]
            ssq = jnp.sum(qn * qn + qr * qr, axis=-1, keepdims=True)
            rsq = lax.rsqrt(ssq * (1.0 / QK_DIM) + EPS)
            q_ref[0, h, rs, :NOPE_DIM] = (qn * rsq * gq_n).astype(BF16)
            q_ref[0, h, rs, NOPE_DIM:] = _rope(qr * rsq * gq_r, cos, sin).astype(BF16)
            kv = kvs[c][h]
            kn = kv[:, :NOPE_DIM]
            ssk = jnp.sum(kn * kn, axis=-1, keepdims=True) + kpe_ssq
            rsk = lax.rsqrt(ssk * (1.0 / QK_DIM) + EPS)
            k_ref[0, h, rs, :NOPE_DIM] = (kn * rsk * gk_n).astype(BF16)
            k_ref[0, h, rs, NOPE_DIM:] = (kpe_rot * rsk).astype(BF16)
            v_ref[0, h, 0, :, rs] = kv[:, NOPE_DIM:].T.astype(BF16)


def _mla_proj(x, g, wd, gcq, gckv, wuq, wukv, gq, gk, cos, sin, *, heads, tm, rows, tk):
    B, S, D = x.shape
    q_lora, kv_lora = gcq.shape[1], gckv.shape[1]
    const = lambda b, i: (0, 0)
    once = pl.Buffered(1)
    kern = functools.partial(_mla_proj_kernel, heads=heads, q_lora=q_lora, kv_lora=kv_lora, rows=rows)
    return pl.pallas_call(
        kern,
        out_shape=(
            jax.ShapeDtypeStruct((B, heads, S, QK_PAD), BF16),
            jax.ShapeDtypeStruct((B, heads, S, QK_PAD), BF16),
            jax.ShapeDtypeStruct((B, heads, S // tk, V_DIM, tk), BF16),
        ),
        grid=(B, S // tm),
        in_specs=[
            pl.BlockSpec((1, tm, D), lambda b, i: (b, i, 0)),
            pl.BlockSpec(g.shape, const),
            pl.BlockSpec(wd.shape, const, pipeline_mode=once),
            pl.BlockSpec(gcq.shape, const),
            pl.BlockSpec(gckv.shape, const),
            pl.BlockSpec(wuq.shape, const, pipeline_mode=once),
            pl.BlockSpec(wukv.shape, const, pipeline_mode=once),
            pl.BlockSpec(gq.shape, const),
            pl.BlockSpec(gk.shape, const),
            pl.BlockSpec((tm, LANES), lambda b, i: (i, 0)),
            pl.BlockSpec((tm, LANES), lambda b, i: (i, 0)),
        ],
        out_specs=(
            pl.BlockSpec((1, heads, tm, QK_PAD), lambda b, i: (b, 0, i, 0)),
            pl.BlockSpec((1, heads, tm, QK_PAD), lambda b, i: (b, 0, i, 0)),
            pl.BlockSpec((1, heads, 1, V_DIM, tm), lambda b, i: (b, 0, i // (tk // tm), 0, i % (tk // tm))),
        ),
        compiler_params=_params("parallel", "parallel"),
        name="mla_proj",
    )(x, g, wd, gcq, gckv, wuq, wukv, gq, gk, cos, sin)


def _mla_attn_kernel(q_ref, k_ref, vt_ref, o_ref, m_ref, l_ref, acc_ref, s_ref, *, tq, tk, sub):
    qi = pl.program_id(2)
    n = tq // sub
    diag_chunks = tq // tk
    assert diag_chunks % 2 == 0
    m_ref[...] = jnp.full(m_ref.shape, MASKED, F32)
    l_ref[...] = jnp.zeros(l_ref.shape, F32)
    acc_ref[...] = jnp.zeros(acc_ref.shape, F32)

    def scores(c, slot, first_chain=0):
        k = k_ref[0, 0, pl.ds(pl.multiple_of(c * tk, tk), tk), :]
        for sb in range(first_chain, n):
            qs = slice(sb * sub, (sb + 1) * sub)
            s_ref[slot, :, qs] = _dot_nt(k, q_ref[0, 0, qs, :])

    def consume(c, slot, diag=None):
        for sb in range(n):
            qs = slice(sb * sub, (sb + 1) * sub)
            keys = tk if diag is None else min(tk, (sb + 1) * sub - diag * tk)
            if keys <= 0:
                continue
            st = s_ref[slot, :keys, qs]
            if diag is not None and diag * tk + keys - 1 > sb * sub:
                key = lax.broadcasted_iota(jnp.int32, st.shape, 0) + diag * tk
                qry = lax.broadcasted_iota(jnp.int32, st.shape, 1) + sb * sub
                st = jnp.where(key <= qry, st, MASKED)
            m_old = m_ref[:, qs]
            m_new = jnp.maximum(m_old, jnp.max(st, axis=0, keepdims=True))
            alpha = jnp.exp2(m_old - m_new)
            p = jnp.exp2(st - m_new)
            l_ref[:, qs] = alpha * l_ref[:, qs] + jnp.sum(p, axis=0, keepdims=True)
            m_ref[:, qs] = m_new
            acc_ref[:, qs] = alpha * acc_ref[:, qs] + _dot(vt_ref[0, 0, c, :, :keys], p.astype(BF16))

    chunks = qi * diag_chunks
    scores(0, 0)

    def body(t, carry):
        scores(2 * t + 1, 1)
        consume(2 * t, 0)
        scores(2 * t + 2, 0)
        consume(2 * t + 1, 1)
        return carry

    lax.fori_loop(0, chunks // 2, body, 0)
    for dc in range(diag_chunks):
        if dc + 1 < diag_chunks:
            scores(chunks + dc + 1, (dc + 1) % 2, first_chain=(dc + 1) * tk // sub)
        consume(chunks + dc, dc % 2, diag=dc)
    o_ref[0] = (acc_ref[...] / l_ref[...]).T.astype(o_ref.dtype)


def _mla_attn(q, k, vt, *, tq, sub):
    B, H, S, _ = q.shape
    nc, _, tk = vt.shape[2:]
    return pl.pallas_call(
        functools.partial(_mla_attn_kernel, tq=tq, tk=tk, sub=sub),
        out_shape=jax.ShapeDtypeStruct((B, S, H * V_DIM), BF16),
        grid=(B, H, S // tq),
        in_specs=[
            pl.BlockSpec((1, 1, tq, QK_PAD), lambda b, h, i: (b, h, i, 0)),
            pl.BlockSpec((1, 1, S, QK_PAD), lambda b, h, i: (b, h, 0, 0)),
            pl.BlockSpec((1, 1, nc, V_DIM, tk), lambda b, h, i: (b, h, 0, 0, 0)),
        ],
        out_specs=pl.BlockSpec((1, tq, V_DIM), lambda b, h, i: (b, i, h)),
        scratch_shapes=[
            pltpu.VMEM((1, tq), F32),
            pltpu.VMEM((1, tq), F32),
            pltpu.VMEM((V_DIM, tq), F32),
            pltpu.VMEM((2, tk, tq), F32),
        ],
        compiler_params=_params("parallel", "parallel", "arbitrary"),
        name="mla_attn",
    )(q, k, vt)


def _out_proj_kernel(x_ref, o_ref, w_ref, out_ref):
    out_ref[...] = x_ref[...] + _dot(o_ref[...], w_ref[...])


def _out_proj(x, o, w, *, tm):
    T, D = x.shape
    return pl.pallas_call(
        _out_proj_kernel,
        out_shape=jax.ShapeDtypeStruct((T, D), F32),
        grid=(T // tm,),
        in_specs=[
            pl.BlockSpec((tm, D), lambda i: (i, 0)),
            pl.BlockSpec((tm, o.shape[1]), lambda i: (i, 0)),
            pl.BlockSpec(w.shape, lambda i: (0, 0)),
        ],
        out_specs=pl.BlockSpec((tm, D), lambda i: (i, 0)),
        compiler_params=_params("parallel"),
        name="out_proj",
    )(x, o, w)


def _dil_qkv_kernel(x_ref, g_ref, w_ref, gain_ref, o0_ref, o1_ref, o2_ref, xn_ref, y_ref):
    j = pl.program_id(1)
    tm = x_ref.shape[0]

    @pl.when(j == 0)
    def _():
        xn_ref[...] = (_rms(x_ref[...]) * g_ref[...]).astype(BF16)

    is_v = j % 3 == 2
    pair_w = 2 * DIL_HEAD_DIM

    for group, (o_ref, (_, d)) in enumerate(zip((o0_ref, o1_ref, o2_ref), DIL_PAIRS)):
        @pl.when(j // 3 == group)
        def _(o_ref=o_ref, d=d):
            xn = xn_ref[...]
            ys = [_dot(xn, w_ref[:, c * pair_w:(c + 1) * pair_w]) for c in range(DIL_HEADS // 2)]
            for h in range(DIL_HEADS):
                cols = slice(h * DIL_HEAD_DIM, (h + 1) * DIL_HEAD_DIM)
                y = ys[h // 2][:, (h % 2) * DIL_HEAD_DIM:(h % 2 + 1) * DIL_HEAD_DIM]
                rs = lax.rsqrt(jnp.mean(y * y, axis=-1, keepdims=True) + EPS)
                z = y * jnp.where(is_v, 1.0, rs) * gain_ref[0]
                if d == 1:
                    o_ref[0, 0, 0, :, cols] = z.astype(BF16)
                elif d % (2 * SUBLANES) == 0:
                    halves = z.reshape(tm // (2 * SUBLANES), 2, SUBLANES, DIL_HEAD_DIM)
                    for half in range(2):
                        y_ref[h, half * (tm // 2):(half + 1) * (tm // 2)] = (
                            halves[:, half].reshape(tm // 2, DIL_HEAD_DIM))
                    for r in range(d):
                        chunk, row = divmod(r % (2 * SUBLANES), SUBLANES)
                        start = chunk * (tm // 2) + (r // (2 * SUBLANES)) * SUBLANES + row
                        o_ref[0, 0, r, :, cols] = y_ref[h, pl.ds(start, tm // d, stride=d // 2), :].astype(BF16)
                else:
                    y_ref[h] = z
                    for r in range(d):
                        o_ref[0, 0, r, :, cols] = y_ref[h, pl.ds(r, tm // d, stride=d), :].astype(BF16)


def _dil_qkv(x, g, w, gains, *, batch, tm):
    T, D = x.shape
    S = T // batch
    tiles = S // tm
    wcol = lambda i, j: (0, (j % 3) * DIL_GROUPS + j // 3)
    out_shapes, out_specs = [], []
    for group, (_, d) in enumerate(DIL_PAIRS):
        out_shapes.append(jax.ShapeDtypeStruct((3, batch, d, S // d, DIL_GROUP_W), BF16))
        out_specs.append(pl.BlockSpec(
            (1, 1, d, tm // d, DIL_GROUP_W),
            lambda i, j, group=group: (jnp.clip(j - 3 * group, 0, 2), i // tiles, 0, i % tiles, 0)))
    return pl.pallas_call(
        _dil_qkv_kernel,
        out_shape=tuple(out_shapes),
        grid=(T // tm, 3 * DIL_GROUPS),
        in_specs=[
            pl.BlockSpec((tm, D), lambda i, j: (i, 0)),
            pl.BlockSpec((1, D), lambda i, j: (0, 0)),
            pl.BlockSpec((D, DIL_GROUP_W), wcol),
            pl.BlockSpec((1, 1, DIL_HEAD_DIM), lambda i, j: (j % 3, 0, 0)),
        ],
        out_specs=tuple(out_specs),
        scratch_shapes=[pltpu.VMEM((tm, D), BF16), pltpu.VMEM((DIL_HEADS, tm, DIL_HEAD_DIM), F32)],
        compiler_params=_params("arbitrary", "arbitrary"),
        name="dil_qkv",
    )(x, g, w, gains)


def _dil_attn_kernel(q_ref, k_ref, kp_ref, v_ref, vp_ref, bias_ref, o_ref, lse_ref, *, tl):
    i = pl.program_id(2)
    first_pen = jnp.where(i == 0, MASKED, 0.0).astype(F32)
    lane = lax.broadcasted_iota(jnp.int32, (DIL_BLK, LANES), 1)
    q_ref, k_ref, kp_ref, v_ref, vp_ref = (t.at[0, 0, 0] for t in (q_ref, k_ref, kp_ref, v_ref, vp_ref))
    heads = [slice(h * DIL_HEAD_DIM, (h + 1) * DIL_HEAD_DIM) for h in range(DIL_HEADS)]
    def block_scores(qb):
        rows = slice(qb * DIL_BLK, (qb + 1) * DIL_BLK)
        both = slice((qb - 1) * DIL_BLK, (qb + 1) * DIL_BLK)
        ss = []
        for h, cols in enumerate(heads):
            q = q_ref[rows, cols]
            if qb == 0:
                s_prev = _dot_nt(q, kp_ref[:, cols]) + first_pen
                s = jnp.concatenate([s_prev, _dot_nt(q, k_ref[rows, cols])], axis=1)
            else:
                s = _dot_nt(q, k_ref[both, cols])
            ss.append(s + bias_ref[h])
        return ss

    for qb in range(tl // DIL_BLK):
        rows = slice(qb * DIL_BLK, (qb + 1) * DIL_BLK)
        both = slice((qb - 1) * DIL_BLK, (qb + 1) * DIL_BLK)
        ss = block_scores(qb)
        stats = []
        for s in ss:
            m = jnp.max(s, axis=-1, keepdims=True)
            p = jnp.exp(s - m)
            stats.append((m, jnp.sum(p, axis=-1, keepdims=True), p.astype(BF16)))
        lse_tile = jnp.zeros((DIL_BLK, LANES), F32)
        for h, (cols, (m, l, pb)) in enumerate(zip(heads, stats)):
            if qb == 0:
                o = _dot(pb[:, :DIL_BLK], vp_ref[:, cols]) + _dot(pb[:, DIL_BLK:], v_ref[rows, cols])
            else:
                o = _dot(pb, v_ref[both, cols])
            o_ref[0, 0, rows, cols] = (o / l).astype(o_ref.dtype)
            lse_tile = jnp.where(lane == h, m + jnp.log(l), lse_tile)
        lse_ref[0, 0, rows, :] = lse_tile


def _dil_attn(qkv, bias, *, tl):
    _, B, d, L, _ = qkv.shape
    bpt = tl // DIL_BLK
    cur = lambda which: pl.BlockSpec((1, 1, 1, tl, DIL_GROUP_W), lambda b, r, i: (which, b, r, i, 0))
    prev = lambda which: pl.BlockSpec((1, 1, 1, DIL_BLK, DIL_GROUP_W),
                                      lambda b, r, i: (which, b, r, jnp.maximum(i * bpt - 1, 0), 0))
    return pl.pallas_call(
        functools.partial(_dil_attn_kernel, tl=tl),
        out_shape=(
            jax.ShapeDtypeStruct((B, d, L, DIL_GROUP_W), BF16),
            jax.ShapeDtypeStruct((B, d, L, LANES), F32),
        ),
        grid=(B, d, L // tl),
        in_specs=[cur(0), cur(1), prev(1), cur(2), prev(2),
                  pl.BlockSpec(bias.shape, lambda b, r, i: (0, 0, 0))],
        out_specs=(
            pl.BlockSpec((1, 1, tl, DIL_GROUP_W), lambda b, r, i: (b, r, i, 0)),
            pl.BlockSpec((1, 1, tl, LANES), lambda b, r, i: (b, r, i, 0)),
        ),
        compiler_params=_params("parallel", "parallel", "arbitrary"),
        name=f"dil_attn_d{d}",
    )(qkv, qkv, qkv, qkv, qkv, bias)


def _dil_out_kernel(x_ref, o0_ref, o1_ref, o2_ref, l0_ref, l1_ref, l2_ref, w_ref, out_ref,
                    u_ref, lt_ref):
    tm = x_ref.shape[0]
    for gi, (o_ref, l_ref, (_, d)) in enumerate(zip((o1_ref, o2_ref), (l1_ref, l2_ref), DIL_PAIRS[1:])):
        for r in range(d):
            rows = pl.ds(r, tm // d, stride=d)
            lt_ref[gi, rows, :] = l_ref[0, r]
            for h in range(DIL_HEADS):
                u_ref[gi, h, rows, :] = o_ref[0, r, :, h * DIL_HEAD_DIM:(h + 1) * DIL_HEAD_DIM].astype(F32)
    lses = (l0_ref[0, 0], lt_ref[0], lt_ref[1])
    mx = jnp.maximum(jnp.maximum(lses[0], lses[1]), lses[2])
    es = [jnp.exp(t - mx) for t in lses]
    den = es[0] + es[1] + es[2]
    ws = [e / den for e in es]
    out = x_ref[...]
    for pair in range(DIL_HEADS // 2):
        merged = []
        for h in (2 * pair, 2 * pair + 1):
            acc = ws[0][:, h:h + 1] * o0_ref[0, 0, :, h * DIL_HEAD_DIM:(h + 1) * DIL_HEAD_DIM].astype(F32)
            for gi in range(DIL_GROUPS - 1):
                acc = acc + ws[gi + 1][:, h:h + 1] * u_ref[gi, h]
            merged.append(acc.astype(BF16))
        rows = slice(2 * pair * DIL_HEAD_DIM, 2 * (pair + 1) * DIL_HEAD_DIM)
        out = out + _dot(jnp.concatenate(merged, axis=1), w_ref[rows, :])
    out_ref[...] = out


def _dil_out(x, os, lses, w, *, batch, tm):
    T, D = x.shape
    tiles = T // batch // tm
    row = lambda i: (i, 0)
    split = lambda d, width: pl.BlockSpec((1, d, tm // d, width), lambda i: (i // tiles, 0, i % tiles, 0))
    return pl.pallas_call(
        _dil_out_kernel,
        out_shape=jax.ShapeDtypeStruct((T, D), F32),
        grid=(T // tm,),
        in_specs=[pl.BlockSpec((tm, D), row)]
        + [split(d, DIL_GROUP_W) for _, d in DIL_PAIRS]
        + [split(d, LANES) for _, d in DIL_PAIRS]
        + [pl.BlockSpec(w.shape, lambda i: (0, 0))],
        out_specs=pl.BlockSpec((tm, D), row),
        scratch_shapes=[
            pltpu.VMEM((DIL_GROUPS - 1, DIL_HEADS, tm, DIL_HEAD_DIM), F32),
            pltpu.VMEM((DIL_GROUPS - 1, tm, LANES), F32),
        ],
        compiler_params=_params("parallel"),
        name="dil_out",
    )(x, *os, *lses, w)


def _pad_rope(t):
    z = jnp.zeros(t.shape[:-1] + (LANES // 2 - ROPE_HALF,), t.dtype)
    return jnp.concatenate([t[..., :ROPE_HALF], z, t[..., ROPE_HALF:], z], axis=-1)


def _pad_qk(t):
    return jnp.concatenate([t[..., :NOPE_DIM], _pad_rope(t[..., NOPE_DIM:])], axis=-1)


def _rope_tables(S):
    inv = 1.0 / (ROPE_THETA ** (jnp.arange(0, ROPE_DIM, 2, dtype=F32) / ROPE_DIM))
    ang = jnp.arange(S, dtype=F32)[:, None] * inv[None, :]
    cos, sin = jnp.cos(ang), jnp.sin(ang)
    return _pad_rope(jnp.concatenate([cos, cos], -1)), _pad_rope(jnp.concatenate([-sin, sin], -1))


def _dil_bias(group, dilation):
    total = DIL_GROUPS * DIL_HEADS
    slopes = 2.0 ** (-8.0 * np.arange(1, total + 1, dtype=np.float32) / total)
    slopes = slopes.reshape(DIL_GROUPS, DIL_HEADS)[group]
    window = DIL_PAIRS[group][0] // dilation
    dist = np.arange(DIL_BLK)[:, None] + DIL_BLK - np.arange(2 * DIL_BLK)[None, :]
    ok = (dist >= 0) & (dist <= window)
    bias = -slopes[:, None, None] * (dilation * dist).astype(np.float32)[None]
    return jnp.asarray(np.where(ok[None], bias, np.float32(MASKED)), dtype=F32)


def _tile(n, want):
    t = min(n, want)
    assert n % t == 0, (n, want)
    return t


def kernel(x, ffn1_norm, ffn1_w_in, ffn1_w_out, mix_norm, ffn2_norm, ffn2_w_in, ffn2_w_out,
           mla_w_down, mla_g_cq, mla_g_ckv, mla_w_uq, mla_w_ukv, mla_g_qn, mla_g_kn, mla_w_o,
           dil_w_qkv, dil_g_qn, dil_g_kn, dil_w_o):
    B, S, D = x.shape
    T = B * S
    depth = ffn1_norm.shape[0]
    F = ffn1_w_out.shape[1]
    tm = _tile(T, 512)
    tf = _tile(F, 512)
    row = lambda v: v.reshape(1, -1).astype(F32)

    ffn_seq = [(norm, w_in, w_out, i) for i in range(depth)
               for norm, w_in, w_out in ((ffn1_norm, ffn1_w_in, ffn1_w_out), (ffn2_norm, ffn2_w_in, ffn2_w_out))]
    ffn_state = {"n": 0, "w": (ffn1_w_in[0].astype(BF16), ffn1_w_out[0].astype(BF16))}

    def ffn(xt, extra_casts=()):
        n = ffn_state["n"]
        norm, _, _, layer = ffn_seq[n]
        casts = list(extra_casts)
        if n + 1 < len(ffn_seq):
            _, nwi, nwo, nlayer = ffn_seq[n + 1]
            casts = [(nwi, nlayer, True), (nwo, nlayer, False)] + casts
        out, cast = _ffn(xt, row(norm[layer]), *ffn_state["w"], casts, tm=_tile(T, 1024), tf=tf)
        ffn_state.update(n=n + 1, w=cast[:2])
        return out, cast[2:] if n + 1 < len(ffn_seq) else cast

    xt = x.reshape(T, D)
    for i in range(depth):
        j = i // 2
        xt, extra = ffn(xt, [(dil_w_qkv, j, True)] if i % 2 == 1 else [])
        if i % 2 == 0:
            q_lora, kv_lora = mla_g_cq.shape[1], mla_g_ckv.shape[1]
            heads = mla_w_uq.shape[2] // QK_DIM
            wd = mla_w_down[j]
            wd = jnp.concatenate([wd[:, :q_lora + kv_lora], _pad_rope(wd[:, q_lora + kv_lora:])], -1)
            wuq = _pad_qk(mla_w_uq[j].reshape(q_lora, heads, QK_DIM)).reshape(q_lora, heads * QK_PAD)
            cos, sin = _rope_tables(S)
            q, k, vt = _mla_proj(
                xt.reshape(B, S, D), row(mix_norm[i]), wd.astype(BF16), row(mla_g_cq[j]), row(mla_g_ckv[j]),
                wuq.astype(BF16), mla_w_ukv[j].astype(BF16),
                row(_pad_qk(mla_g_qn[j]) * (LOG2E / math.sqrt(QK_DIM))), row(_pad_qk(mla_g_kn[j])),
                cos, sin, heads=heads, tm=_tile(S, 512), rows=256, tk=_tile(S, 512))
            o = _mla_attn(q, k, vt, tq=_tile(S, 2048), sub=256)
            xt = _out_proj(xt, o.reshape(T, heads * V_DIM), mla_w_o[j].astype(BF16), tm=tm)
        else:
            gains = jnp.stack([dil_g_qn[j] * (1.0 / math.sqrt(DIL_HEAD_DIM)), dil_g_kn[j],
                               jnp.ones_like(dil_g_kn[j])]).reshape(3, 1, DIL_HEAD_DIM)
            qkvs = _dil_qkv(xt, row(mix_norm[i]), extra[0], gains, batch=B, tm=_tile(S, 1024))
            os, lses = [], []
            for g, (_, dilation) in enumerate(DIL_PAIRS):
                o_g, lse_g = _dil_attn(qkvs[g], _dil_bias(g, dilation), tl=_tile(S // dilation, 512))
                os.append(o_g)
                lses.append(lse_g)
            xt = _dil_out(xt, os, lses, dil_w_o[j].astype(BF16), batch=B, tm=_tile(S, 512))
        xt, _ = ffn(xt)
    return xt.reshape(B, S, D)
```
